```python
import math
import jax, jax.numpy as jnp
from jax import lax
import numpy as np

D_MODEL = 1024
BATCH = 16
SEQ = 256
DEPTH = 2
DEC_BATCH = 8
DEC_SEQ = 1024
PAST_LEN = 512

GRID_W = 64
EPS = 1e-6
DN_HEADS = 4
DN_HEAD_DIM = 128
DN_WIDTH = DN_HEADS * DN_HEAD_DIM
DN_CHUNK = 64
SHORT_CONV = 3
SSM_WIDTH = 512
SSM_GROUP_CH = 16
SSM_GROUPS = SSM_WIDTH // SSM_GROUP_CH
SSM_STATE = 64
POOL_WINDOWS = (2, 4, 8, 16)
POOL_WIDTH = 512
POOL_GROUP = POOL_WIDTH // len(POOL_WINDOWS)
N_BRANCH = 3
D_FF = 2816
FFN_CONV = 3
IN_SPLITS = (DN_WIDTH, DN_WIDTH, DN_WIDTH, DN_WIDTH, 2 * DN_HEADS, 2 * DN_HEADS,
             SSM_WIDTH, POOL_WIDTH, N_BRANCH * D_MODEL)
IN_COLS = sum(IN_SPLITS)

kernel_name = "hybrid_dit_deltanet_s5_pool_step"


def rmsnorm(x, g):
    xf = x.astype(jnp.float32)
    y = xf * lax.rsqrt(jnp.mean(xf * xf, axis=-1, keepdims=True) + EPS)
    return (y * g.astype(jnp.float32)).astype(x.dtype)


def l2norm(x):
    return x * lax.rsqrt(jnp.sum(x * x, axis=-1, keepdims=True) + EPS)


def dwconv(x, w):
    k = w.shape[0]
    return lax.conv_general_dilated(
        x, w[:, None, :].astype(x.dtype), window_strides=(1,),
        padding=[(k // 2, k // 2)], dimension_numbers=("NWC", "WIO", "NWC"),
        feature_group_count=x.shape[-1])


def grid_pos_embed(rows, dim):
    quarter = dim // 4
    omega = 1.0 / (10000.0 ** (jnp.arange(quarter, dtype=jnp.float32) / quarter))
    r = jnp.broadcast_to(jnp.arange(rows, dtype=jnp.float32)[:, None], (rows, GRID_W)).reshape(-1)
    col = jnp.broadcast_to(jnp.arange(GRID_W, dtype=jnp.float32)[None, :], (rows, GRID_W)).reshape(-1)

    def sincos(p):
        ang = p[:, None] * omega[None, :]
        return jnp.concatenate([jnp.sin(ang), jnp.cos(ang)], axis=-1)

    return jnp.concatenate([sincos(r), sincos(col)], axis=-1)


def gated_delta_chunked(q, k, v, g, beta, s0):
    bsz, L, H, _ = q.shape
    dv = v.shape[-1]
    n = L // DN_CHUNK

    def chunks(t):
        return t.reshape(bsz, n, DN_CHUNK, H, -1).transpose(1, 0, 3, 2, 4)

    qc, kc, vc = chunks(q), chunks(k), chunks(v)
    gc = jnp.cumsum(g.reshape(bsz, n, DN_CHUNK, H).transpose(1, 0, 3, 2), axis=-1)
    bc = beta.reshape(bsz, n, DN_CHUNK, H).transpose(1, 0, 3, 2)[..., None]
    idx = jnp.arange(DN_CHUNK)
    causal = idx[:, None] >= idx[None, :]
    strict = idx[:, None] > idx[None, :]
    decay = jnp.exp(jnp.where(causal, gc[..., :, None] - gc[..., None, :], -jnp.inf))
    k_beta = kc * bc
    a_mat = jnp.where(strict, jnp.einsum("nbhid,nbhjd->nbhij", k_beta, kc) * decay, 0.0)
    tri = a_mat + jnp.eye(DN_CHUNK, dtype=a_mat.dtype)
    u_val = lax.linalg.triangular_solve(tri, vc * bc, left_side=True, lower=True, unit_diagonal=True)
    w_key = lax.linalg.triangular_solve(tri, k_beta * jnp.exp(gc)[..., None],
                                        left_side=True, lower=True, unit_diagonal=True)
    qk = jnp.einsum("nbhid,nbhjd->nbhij", qc, kc) * decay

    def step(S, inp):
        q_i, k_i, u_i, w_i, g_i, qk_i = inp
        v_new = u_i - jnp.einsum("bhik,bhkv->bhiv", w_i, S)
        o_i = (jnp.einsum("bhik,bhkv->bhiv", q_i * jnp.exp(g_i)[..., None], S)
               + jnp.einsum("bhij,bhjv->bhiv", qk_i, v_new))
        g_last = g_i[..., -1:]
        S = (S * jnp.exp(g_last)[..., None]
             + jnp.einsum("bhik,bhiv->bhkv", k_i * jnp.exp(g_last - g_i)[..., None], v_new))
        return S, o_i

    s_final, o = lax.scan(step, s0.astype(jnp.float32), (qc, kc, u_val, w_key, gc, qk))
    return o.transpose(1, 0, 3, 2, 4).reshape(bsz, L, H, dv), s_final


def deltanet_mixer(q, k, v, z, a, b, conv_w, a_log, dt_bias, norm_g, s0):
    bsz, L, _ = q.shape
    qkv = jax.nn.silu(dwconv(jnp.concatenate([q, k, v], axis=-1), conv_w)).astype(jnp.float32)
    qh, kh, vh = [t.reshape(bsz, L, DN_HEADS, DN_HEAD_DIM) for t in jnp.split(qkv, 3, axis=-1)]
    qh = l2norm(qh) * (DN_HEAD_DIM ** -0.5)
    kh = l2norm(kh)
    a = a.astype(jnp.float32).reshape(bsz, L, 2, DN_HEADS)
    b = b.astype(jnp.float32).reshape(bsz, L, 2, DN_HEADS)
    g = -jnp.exp(a_log.astype(jnp.float32)) * jax.nn.softplus(a + dt_bias.astype(jnp.float32))
    beta = jax.nn.sigmoid(b)
    o_f, s_f = gated_delta_chunked(qh, kh, vh, g[:, :, 0], beta[:, :, 0], s0[:, 0])
    o_b, s_b = gated_delta_chunked(qh[:, ::-1], kh[:, ::-1], vh[:, ::-1],
                                   g[:, ::-1, 1], beta[:, ::-1, 1], s0[:, 1])
    o = o_f + o_b[:, ::-1]
    o = rmsnorm(o, norm_g) * jax.nn.silu(z.astype(jnp.float32).reshape(bsz, L, DN_HEADS, DN_HEAD_DIM))
    return o.reshape(bsz, L, DN_WIDTH).astype(q.dtype), jnp.stack([s_f, s_b], axis=1)


def s5_scan(u, lam_re, lam_im, log_step, b_re, b_im, h0_re, h0_im):
    step = jnp.exp(log_step)[:, None]
    mag = jnp.exp(lam_re * step)
    ang = lam_im * step
    lb_re, lb_im = mag * jnp.cos(ang), mag * jnp.sin(ang)
    nr, ni = lb_re - 1.0, lb_im
    den = lam_re * lam_re + lam_im * lam_im
    f_re = (nr * lam_re + ni * lam_im) / den
    f_im = (ni * lam_re - nr * lam_im) / den
    bb_re = f_re[..., None] * b_re - f_im[..., None] * b_im
    bb_im = f_re[..., None] * b_im + f_im[..., None] * b_re
    bu_re = jnp.einsum("blgc,gpc->blgp", u, bb_re)
    bu_im = jnp.einsum("blgc,gpc->blgp", u, bb_im)
    bu_re = bu_re.at[:, 0].add(lb_re * h0_re - lb_im * h0_im)
    bu_im = bu_im.at[:, 0].add(lb_re * h0_im + lb_im * h0_re)
    a_re = jnp.broadcast_to(lb_re, bu_re.shape)
    a_im = jnp.broadcast_to(lb_im, bu_im.shape)

    def combine(e1, e2):
        a1r, a1i, b1r, b1i = e1
        a2r, a2i, b2r, b2i = e2
        return (a2r * a1r - a2i * a1i, a2r * a1i + a2i * a1r,
                a2r * b1r - a2i * b1i + b2r, a2r * b1i + a2i * b1r + b2i)

    _, _, x_re, x_im = lax.associative_scan(combine, (a_re, a_im, bu_re, bu_im), axis=1)
    return x_re, x_im


def s5_mixer(u, lam_re, lam_im, log_step, b_re, b_im, c_re, c_im, d_skip, glu_w, glu_b, h0_re, h0_im):
    bsz, L, _ = u.shape
    f32 = jnp.float32
    uf = u.astype(f32)
    ug = uf.reshape(bsz, L, SSM_GROUPS, SSM_GROUP_CH)
    lam_re, lam_im, log_step = lam_re.astype(f32), lam_im.astype(f32), log_step.astype(f32)
    b_re, b_im = b_re.astype(f32), b_im.astype(f32)
    h0_re, h0_im = h0_re.astype(f32), h0_im.astype(f32)
    xf_re, xf_im = s5_scan(ug, lam_re[0], lam_im[0], log_step[0], b_re, b_im, h0_re[:, 0], h0_im[:, 0])
    xb_re, xb_im = s5_scan(ug[:, ::-1], lam_re[1], lam_im[1], log_step[1], b_re, b_im,
                           h0_re[:, 1], h0_im[:, 1])
    fin_re = jnp.stack([xf_re[:, -1], xb_re[:, -1]], axis=1)
    fin_im = jnp.stack([xf_im[:, -1], xb_im[:, -1]], axis=1)
    x_re = xf_re + xb_re[:, ::-1]
    x_im = xf_im + xb_im[:, ::-1]
    y = (jnp.einsum("blgp,gcp->blgc", x_re, c_re.astype(f32))
         - jnp.einsum("blgp,gcp->blgc", x_im, c_im.astype(f32)))
    y = y.reshape(bsz, L, SSM_WIDTH) + d_skip.astype(f32) * uf
    y = jax.nn.gelu(y)
    y = y * jax.nn.sigmoid(y @ glu_w.astype(f32) + glu_b.astype(f32))
    return y.astype(u.dtype), fin_re, fin_im


def pool_mixer(u, pool_w, pool_scale):
    bsz, L, _ = u.shape
    uf = u.astype(jnp.float32)
    cs = jnp.concatenate([jnp.zeros((bsz, 1, POOL_WIDTH), jnp.float32), jnp.cumsum(uf, axis=1)], axis=1)
    t = jnp.arange(L)
    outs = []
    for gi, w in enumerate(POOL_WINDOWS):
        csg = cs[..., gi * POOL_GROUP:(gi + 1) * POOL_GROUP]
        lo = jnp.clip(t - w // 2, 0, L)
        hi = jnp.clip(t + w // 2, 0, L)
        s = jnp.take(csg, hi, axis=1) - jnp.take(csg, lo, axis=1)
        outs.append(s / (hi - lo).astype(jnp.float32)[None, :, None])
    pooled = (jnp.concatenate(outs, axis=-1) - uf).reshape(bsz, L, len(POOL_WINDOWS), POOL_GROUP)
    mixed = jnp.einsum("blgc,gcd->blgd", pooled, pool_w.astype(jnp.float32)).reshape(bsz, L, POOL_WIDTH)
    return (mixed * pool_scale.astype(jnp.float32)).astype(u.dtype)


def conv_ffn(h, w_up, conv_w, w_down):
    hu = dwconv(h @ w_up, conv_w)
    gate, val = jnp.split(hu, 2, axis=-1)
    return (jax.nn.silu(gate) * val) @ w_down


def trunk_layer(x, cond, s_dn, s_re, s_im, norm1_g, norm2_g, w_ada, b_ada, w_in, dn_conv, dn_a_log,
                dn_dt_bias, dn_norm_g, ssm_lambda_re, ssm_lambda_im, ssm_log_step, ssm_b_re, ssm_b_im,
                ssm_c_re, ssm_c_im, ssm_d, ssm_glu_w, ssm_glu_b, pool_w, pool_scale, w_branch_dn,
                w_branch_ssm, w_branch_pool, w_out, ffn_w_up, ffn_conv, ffn_w_down):
    mod = (jax.nn.silu(cond) @ w_ada + b_ada)[:, None, :]
    shift1, scale1, gate1, shift2, scale2, gate2 = jnp.split(mod, 6, axis=-1)
    h = rmsnorm(x, norm1_g) * (1 + scale1) + shift1
    offsets = [int(o) for o in np.cumsum(IN_SPLITS)[:-1]]
    q, k, v, z, dn_a, dn_b, u_ssm, u_pool, gates = jnp.split(h @ w_in, offsets, axis=-1)
    o_dn, st_dn = deltanet_mixer(q, k, v, z, dn_a, dn_b, dn_conv, dn_a_log, dn_dt_bias, dn_norm_g, s_dn)
    o_ssm, st_re, st_im = s5_mixer(u_ssm, ssm_lambda_re, ssm_lambda_im, ssm_log_step, ssm_b_re, ssm_b_im,
                                   ssm_c_re, ssm_c_im, ssm_d, ssm_glu_w, ssm_glu_b, s_re, s_im)
    o_pool = pool_mixer(u_pool, pool_w, pool_scale)
    g_dn, g_ssm, g_pool = jnp.split(jax.nn.sigmoid(gates), N_BRANCH, axis=-1)
    merged = g_dn * (o_dn @ w_branch_dn) + g_ssm * (o_ssm @ w_branch_ssm) + g_pool * (o_pool @ w_branch_pool)
    x = x + gate1 * (merged @ w_out)
    h = rmsnorm(x, norm2_g) * (1 + scale2) + shift2
    x = x + gate2 * conv_ffn(h, ffn_w_up, ffn_conv, ffn_w_down)
    return x, st_dn, st_re, st_im


def setup_inputs(seed: int = 0) -> dict:
    key = jax.random.key(seed)
    ks = jax.random.split(key, 40)
    nrm = jax.random.normal
    D = D_MODEL
    dt = jnp.exp(jax.random.uniform(ks[14], (DEPTH, 2, DN_HEADS), minval=math.log(1e-3), maxval=math.log(1e-1)))
    return {
        "x_prompt": nrm(ks[0], (BATCH, SEQ, D)),
        "x_sample": nrm(ks[1], (DEC_BATCH, DEC_SEQ, D)),
        "state_dn": 0.1 * nrm(ks[2], (DEC_BATCH, DEPTH, 2, DN_HEADS, DN_HEAD_DIM, DN_HEAD_DIM)),
        "state_ssm_re": 0.1 * nrm(ks[3], (DEC_BATCH, DEPTH, 2, SSM_GROUPS, SSM_STATE)),
        "state_ssm_im": 0.1 * nrm(ks[4], (DEC_BATCH, DEPTH, 2, SSM_GROUPS, SSM_STATE)),
        "c": nrm(ks[5], (DEC_BATCH, D)),
        "c_ctx": nrm(ks[6], (D,)),
        "norm1_g": 1.0 + 0.02 * nrm(ks[7], (DEPTH, D)),
        "norm2_g": 1.0 + 0.02 * nrm(ks[8], (DEPTH, D)),
        "w_ada": 0.5 * D ** -0.5 * nrm(ks[9], (DEPTH, D, 6 * D)),
        "b_ada": 0.02 * nrm(ks[10], (DEPTH, 6 * D)),
        "w_in": D ** -0.5 * nrm(ks[11], (DEPTH, D, IN_COLS)),
        "dn_conv": SHORT_CONV ** -0.5 * nrm(ks[12], (DEPTH, SHORT_CONV, 3 * DN_WIDTH)),
        "dn_a_log": jnp.log(jax.random.uniform(ks[13], (DEPTH, 2, DN_HEADS), minval=1.0, maxval=16.0)),
        "dn_dt_bias": dt + jnp.log(-jnp.expm1(-dt)),
        "dn_norm_g": 1.0 + 0.02 * nrm(ks[15], (DEPTH, DN_HEAD_DIM)),
        "ssm_lambda_re": -0.5 + 0.01 * nrm(ks[16], (DEPTH, 2, SSM_GROUPS, SSM_STATE)),
        "ssm_lambda_im": jnp.pi * jnp.arange(SSM_STATE, dtype=jnp.float32)
                         + 0.01 * nrm(ks[17], (DEPTH, 2, SSM_GROUPS, SSM_STATE)),
        "ssm_log_step": jax.random.uniform(ks[18], (DEPTH, 2, SSM_GROUPS),
                                           minval=math.log(1e-3), maxval=math.log(1e-1)),
        "ssm_b_re": (2 * SSM_GROUP_CH) ** -0.5 * nrm(ks[19], (DEPTH, SSM_GROUPS, SSM_STATE, SSM_GROUP_CH)),
        "ssm_b_im": (2 * SSM_GROUP_CH) ** -0.5 * nrm(ks[20], (DEPTH, SSM_GROUPS, SSM_STATE, SSM_GROUP_CH)),
        "ssm_c_re": SSM_STATE ** -0.5 * nrm(ks[21], (DEPTH, SSM_GROUPS, SSM_GROUP_CH, SSM_STATE)),
        "ssm_c_im": SSM_STATE ** -0.5 * nrm(ks[22], (DEPTH, SSM_GROUPS, SSM_GROUP_CH, SSM_STATE)),
        "ssm_d": nrm(ks[23], (DEPTH, SSM_WIDTH)),
        "ssm_glu_w": SSM_WIDTH ** -0.5 * nrm(ks[24], (DEPTH, SSM_WIDTH, SSM_WIDTH)),
        "ssm_glu_b": 0.02 * nrm(ks[25], (DEPTH, SSM_WIDTH)),
        "pool_w": POOL_GROUP ** -0.5 * nrm(ks[26], (DEPTH, len(POOL_WINDOWS), POOL_GROUP, POOL_GROUP)),
        "pool_scale": 1.0 + 0.1 * nrm(ks[27], (DEPTH, POOL_WIDTH)),
        "w_branch_dn": DN_WIDTH ** -0.5 * nrm(ks[28], (DEPTH, DN_WIDTH, D)),
        "w_branch_ssm": SSM_WIDTH ** -0.5 * nrm(ks[29], (DEPTH, SSM_WIDTH, D)),
        "w_branch_pool": POOL_WIDTH ** -0.5 * nrm(ks[30], (DEPTH, POOL_WIDTH, D)),
        "w_out": D ** -0.5 * nrm(ks[31], (DEPTH, D, D)),
        "ffn_w_up": D ** -0.5 * nrm(ks[32], (DEPTH, D, 2 * D_FF)),
        "ffn_conv": FFN_CONV ** -0.5 * nrm(ks[33], (DEPTH, FFN_CONV, 2 * D_FF)),
        "ffn_w_down": D_FF ** -0.5 * nrm(ks[34], (DEPTH, D_FF, D)),
        "final_norm_g": 1.0 + 0.02 * nrm(ks[35], (D,)),
    }


def reference(x_prompt, x_sample, state_dn, state_ssm_re, state_ssm_im, c, c_ctx, norm1_g, norm2_g,
              w_ada, b_ada, w_in, dn_conv, dn_a_log, dn_dt_bias, dn_norm_g, ssm_lambda_re, ssm_lambda_im,
              ssm_log_step, ssm_b_re, ssm_b_im, ssm_c_re, ssm_c_im, ssm_d, ssm_glu_w, ssm_glu_b, pool_w,
              pool_scale, w_branch_dn, w_branch_ssm, w_branch_pool, w_out, ffn_w_up, ffn_conv, ffn_w_down,
              final_norm_g):
    bsz = x_prompt.shape[0]
    rows = x_sample.shape[1] // GRID_W
    cond_ctx = c_ctx[None, :]
    x_ctx = x_prompt
    x_lat = x_sample + grid_pos_embed(rows, D_MODEL).astype(x_sample.dtype)[None]
    zero_dn = jnp.zeros((bsz, 2, DN_HEADS, DN_HEAD_DIM, DN_HEAD_DIM), jnp.float32)
    zero_ssm = jnp.zeros((bsz, 2, SSM_GROUPS, SSM_STATE), jnp.float32)
    new_dn, new_re, new_im = [], [], []
    for l in range(DEPTH):
        def run(x, cond, s_dn, s_re, s_im):
            return trunk_layer(x, cond, s_dn, s_re, s_im, norm1_g[l], norm2_g[l], w_ada[l], b_ada[l], w_in[l],
                               dn_conv[l], dn_a_log[l], dn_dt_bias[l], dn_norm_g[l], ssm_lambda_re[l],
                               ssm_lambda_im[l], ssm_log_step[l], ssm_b_re[l], ssm_b_im[l], ssm_c_re[l],
                               ssm_c_im[l], ssm_d[l], ssm_glu_w[l], ssm_glu_b[l], pool_w[l], pool_scale[l],
                               w_branch_dn[l], w_branch_ssm[l], w_branch_pool[l], w_out[l], ffn_w_up[l],
                               ffn_conv[l], ffn_w_down[l])
        x_ctx, st_dn, st_re, st_im = run(x_ctx, cond_ctx, zero_dn, zero_ssm, zero_ssm)
        new_dn.append(st_dn)
        new_re.append(st_re)
        new_im.append(st_im)
        x_lat, _, _, _ = run(x_lat, c, state_dn[:, l], state_ssm_re[:, l], state_ssm_im[:, l])
    y_prompt = rmsnorm(x_ctx, final_norm_g)
    y_sample = rmsnorm(x_lat, final_norm_g)
    new_state_dn = jnp.stack(new_dn, axis=1)
    new_state_ssm_re = jnp.stack(new_re, axis=1)
    new_state_ssm_im = jnp.stack(new_im, axis=1)
    return (y_prompt, y_sample, new_state_dn, new_state_ssm_re, new_state_ssm_im)
```

```python
import functools

import jax
import jax.numpy as jnp
from jax import lax
from jax.experimental import pallas as pl
from jax.experimental.pallas import tpu as pltpu

F32 = jnp.float32
BF16 = jnp.bfloat16

D = 1024
DEPTH = 2
GRID_W = 64
EPS = 1e-6
HEADS = 4
HD = 128
DNW = HEADS * HD
CH = 128
N_LVL = 7
SSM_W = 512
SSM_G = 32
SSM_GC = 16
SSM_P = 64
SSM_N = SSM_G * SSM_P
SSM_BLK = 4
POOL_WINDOWS = (2, 4, 8, 16)
POOL_W = 512
POOL_G = 128
DFF = 2816
FF_BLK = 256
PAD = 8
SUB = 8
VMEM_LIMIT = 56 << 20


def _cparams(*sem):
    return pltpu.CompilerParams(dimension_semantics=sem, vmem_limit_bytes=VMEM_LIMIT)


def _const_spec(shape):
    zeros = (0,) * len(shape)
    return pl.BlockSpec(shape, lambda *_: zeros, pipeline_mode=pl.Buffered(1))


def _mm(a, b):
    return jnp.dot(a.astype(BF16), b.astype(BF16), preferred_element_type=F32)


def _mm_nt(a, b):
    return lax.dot_general(a.astype(BF16), b.astype(BF16), (((1,), (1,)), ((), ())),
                           preferred_element_type=F32)


def _mm_tn(a, b):
    return lax.dot_general(a.astype(BF16), b.astype(BF16), (((0,), (0,)), ((), ())),
                           preferred_element_type=F32)


def _mm_exact(a, b):
    return jnp.dot(a, b, precision=lax.Precision.HIGHEST, preferred_element_type=F32)


def _silu(x):
    return x * jax.nn.sigmoid(x)


def _rms(x):
    return x * lax.rsqrt(jnp.mean(x * x, axis=-1, keepdims=True) + EPS)


def _ada_kernel(c_ref, w_ref, b_ref, o_ref):
    o_ref[...] = _mm(_silu(c_ref[...]), w_ref[...]) + b_ref[...]


def _ada(conds, w_ada, b_ada):
    n = conds.shape[0]
    tn = 1536
    return pl.pallas_call(
        _ada_kernel,
        grid=(DEPTH, 6 * D // tn),
        in_specs=[pl.BlockSpec((n, D), lambda l, j: (0, 0)),
                  pl.BlockSpec((None, D, tn), lambda l, j: (l, 0, j)),
                  pl.BlockSpec((None, 1, tn), lambda l, j: (l, 0, j))],
        out_specs=pl.BlockSpec((None, n, tn), lambda l, j: (l, 0, j)),
        out_shape=jax.ShapeDtypeStruct((DEPTH, n, 6 * D), F32),
        compiler_params=_cparams("arbitrary", "arbitrary"),
        name="ada",
    )(conds, w_ada, b_ada.reshape(DEPTH, 1, 6 * D))


def _addpos_kernel(x_ref, p_ref, o_ref):
    o_ref[...] = x_ref[...] + p_ref[...]


def _add_pos(x, pos):
    b, l, _ = x.shape
    return pl.pallas_call(
        _addpos_kernel,
        grid=(b,),
        in_specs=[pl.BlockSpec((None, l, D), lambda i: (i, 0, 0)), _const_spec((l, D))],
        out_specs=pl.BlockSpec((None, l, D), lambda i: (i, 0, 0)),
        out_shape=jax.ShapeDtypeStruct(x.shape, F32),
        compiler_params=_cparams("arbitrary"),
        name="add_pos",
    )(x, pos)


def _inproj_kernel(x_ref, mod_ref, g_ref, wm_ref, wab_ref, wsp_ref, wg_ref,
                   qkvz_ref, ab_ref, sp_ref, gate_ref):
    y = _rms(x_ref[...]) * g_ref[...]
    h = (y * (1.0 + mod_ref[0, 1:2, :]) + mod_ref[0, 0:1, :]).astype(BF16)
    qkvz_ref[...] = jnp.dot(h, wm_ref[...], preferred_element_type=F32)
    ab_ref[...] = jnp.dot(h, wab_ref[...], preferred_element_type=F32)
    sp_ref[...] = jnp.dot(h, wsp_ref[...], preferred_element_type=F32)
    gate_ref[...] = jnp.dot(h, wg_ref[...], preferred_element_type=F32)


def _in_proj(x2, mod, seq_len, norm_g, wm, wab, wsp, wg):
    t = x2.shape[0]
    tm = 512
    per_seq = mod.shape[0] > 1
    tiles_per_seq = seq_len // tm
    if per_seq:
        assert seq_len % tm == 0
        mod_map = lambda i: (i // tiles_per_seq, 0, 0)
    else:
        mod_map = lambda i: (0, 0, 0)
    row = lambda w: pl.BlockSpec((tm, w), lambda i: (i, 0))
    return pl.pallas_call(
        _inproj_kernel,
        grid=(t // tm,),
        in_specs=[row(D), pl.BlockSpec((1, 6, D), mod_map), _const_spec((1, D)),
                  _const_spec(wm.shape), _const_spec(wab.shape), _const_spec(wsp.shape),
                  _const_spec(wg.shape)],
        out_specs=[row(4 * DNW), row(128), row(SSM_W + POOL_W), row(3 * D)],
        out_shape=[jax.ShapeDtypeStruct((t, 4 * DNW), F32), jax.ShapeDtypeStruct((t, 128), F32),
                   jax.ShapeDtypeStruct((t, SSM_W + POOL_W), F32),
                   jax.ShapeDtypeStruct((t, 3 * D), F32)],
        compiler_params=_cparams("arbitrary"),
        name="in_proj",
    )(x2, mod, norm_g.reshape(1, D), wm, wab, wsp, wg)


def _dn_kernel(qkv_ref, z_ref, ab_ref, cw_ref, alog_ref, dtb_ref, ng_ref, s0_ref,
               o_ref, sfin_ref,
               pad_s, q_s, k_s, v_s, g_s, gt_s, b_s, o_s, st_s, lm_s):
    L = o_ref.shape[0]
    nc = L // CH

    pad_s[0:PAD, :] = jnp.zeros((PAD, 3 * DNW), F32)
    pad_s[PAD + L:PAD + L + PAD, :] = jnp.zeros((PAD, 3 * DNW), F32)
    pad_s[PAD:PAD + L, :] = qkv_ref[...]
    rc = 256
    for cb in range(3 * HEADS):
        cols = slice(cb * HD, (cb + 1) * HD)
        w = cw_ref[:, cols]
        for r0 in range(0, L, rc):
            y = (w[0:1] * pad_s[PAD - 1 + r0:PAD - 1 + r0 + rc, cols]
                 + w[1:2] * pad_s[PAD + r0:PAD + r0 + rc, cols]
                 + w[2:3] * pad_s[PAD + 1 + r0:PAD + 1 + r0 + rc, cols])
            y = _silu(y)
            h = cb % HEADS
            hc = slice(h * HD, (h + 1) * HD)
            if cb < HEADS:
                y = y * lax.rsqrt(jnp.sum(y * y, axis=-1, keepdims=True) + EPS) * (HD ** -0.5)
                q_s[r0:r0 + rc, hc] = y
            elif cb < 2 * HEADS:
                y = y * lax.rsqrt(jnp.sum(y * y, axis=-1, keepdims=True) + EPS)
                k_s[r0:r0 + rc, hc] = y
            else:
                v_s[r0:r0 + rc, hc] = y

    ab = ab_ref[...]
    xs = ab + dtb_ref[...]
    softplus = jnp.maximum(xs, 0.0) + jnp.log1p(jnp.exp(-jnp.abs(xs)))
    g = -jnp.exp(alog_ref[...]) * softplus
    g_s[...] = g
    gt_s[...] = g.T
    b_s[...] = jax.nn.sigmoid(ab)

    for d in range(2):
        for h in range(HEADS):
            st_s[d * HEADS + h] = s0_ref[d, h]
    o_s[...] = jnp.zeros(o_s.shape, F32)

    ri = lax.broadcasted_iota(jnp.int32, (CH, CH), 0)
    ci = lax.broadcasted_iota(jnp.int32, (CH, CH), 1)
    lo = (ri >= ci).astype(F32)
    up = (ri <= ci).astype(F32)
    eye = (ri == ci).astype(F32)
    for lvl in range(N_LVL):
        pair = ((ri >> (lvl + 1)) == (ci >> (lvl + 1))) & ((ri >> lvl) != (ci >> lvl))
        lm_s[lvl] = pair.astype(F32)

    def chunk_step(c, carry):
        for d in range(2):
            cidx = c if d == 0 else nc - 1 - c
            r0 = pl.multiple_of(cidx * CH, CH)
            rows = pl.ds(r0, CH)
            tri, tri_t = (lo, up) if d == 0 else (up, lo)
            incl = (ri >= ci) if d == 0 else (ri <= ci)
            strict = (ri > ci) if d == 0 else (ri < ci)
            gcol = _mm_exact(tri, g_s[rows, :])
            grow = _mm_exact(gt_s[:, rows], tri_t)
            last = CH - 1 if d == 0 else 0
            for h in range(HEADS):
                s = d * HEADS + h
                hc = slice(h * HD, (h + 1) * HD)
                q = q_s[rows, hc]
                k = k_s[rows, hc]
                v = v_s[rows, hc]
                gc = gcol[:, s:s + 1]
                gr = grow[s:s + 1, :]
                gtot = gcol[last:last + 1, s:s + 1]
                beta = b_s[rows, 2 * HEADS + s:2 * HEADS + s + 1]
                decay = jnp.where(incl, jnp.exp(gc - gr), 0.0)
                eg = jnp.exp(gc)
                kb = k * beta
                a = jnp.where(strict, _mm_nt(kb, k) * decay, 0.0)
                t = eye - a * lm_s[0]
                for lvl in range(1, N_LVL):
                    t = t - _mm(t, _mm(a * lm_s[lvl], t))
                uw = _mm(t, jnp.concatenate([v * beta, kb * eg], axis=1))
                u = uw[:, :HD]
                w = uw[:, HD:]
                qk = _mm_nt(q, k) * decay
                st = st_s[s]
                ws = _mm(jnp.concatenate([w, q * eg], axis=0), st)
                v_new = u - ws[:CH]
                o = ws[CH:] + _mm(qk, v_new)
                st_s[s] = st * jnp.exp(gtot) + _mm_tn(k * jnp.exp(gtot - gc), v_new)
                o_s[rows, hc] = o_s[rows, hc] + o
        return carry

    lax.fori_loop(0, nc, chunk_step, 0)

    for d in range(2):
        for h in range(HEADS):
            sfin_ref[d, h] = st_s[d * HEADS + h]
    for h in range(HEADS):
        hc = slice(h * HD, (h + 1) * HD)
        for r0 in range(0, L, rc):
            z = z_ref[r0:r0 + rc, hc]
            o_ref[r0:r0 + rc, hc] = _rms(o_s[r0:r0 + rc, hc]) * ng_ref[...] * _silu(z)


def _deltanet(qkvz, ab, n_seq, seq_len, conv_w, a_log, dt_bias, norm_g, s0):
    L = seq_len
    pad_row = lambda v: jnp.zeros((1, 128), F32).at[0, :2 * HEADS].set(v.reshape(-1))
    st_spec = pl.BlockSpec((None, 2, HEADS, HD, HD), lambda i: (i, 0, 0, 0, 0))
    return pl.pallas_call(
        _dn_kernel,
        grid=(n_seq,),
        in_specs=[pl.BlockSpec((L, 3 * DNW), lambda i: (i, 0)),
                  pl.BlockSpec((L, DNW), lambda i: (i, 3)),
                  pl.BlockSpec((L, 128), lambda i: (i, 0)),
                  _const_spec((3, 3 * DNW)), _const_spec((1, 128)), _const_spec((1, 128)),
                  _const_spec((1, HD)), st_spec],
        out_specs=[pl.BlockSpec((L, DNW), lambda i: (i, 0)), st_spec],
        out_shape=[jax.ShapeDtypeStruct((n_seq * L, DNW), F32),
                   jax.ShapeDtypeStruct((n_seq, 2, HEADS, HD, HD), F32)],
        scratch_shapes=[pltpu.VMEM((L + 2 * PAD, 3 * DNW), F32),
                        pltpu.VMEM((L, DNW), F32), pltpu.VMEM((L, DNW), F32), pltpu.VMEM((L, DNW), F32),
                        pltpu.VMEM((L, 128), F32), pltpu.VMEM((128, L), F32), pltpu.VMEM((L, 128), F32),
                        pltpu.VMEM((L, DNW), F32), pltpu.VMEM((2 * HEADS, HD, HD), F32),
                        pltpu.VMEM((N_LVL, CH, CH), F32)],
        compiler_params=_cparams("arbitrary"),
        name="deltanet",
    )(qkvz, qkvz, ab, conv_w, pad_row(a_log), pad_row(dt_bias), norm_g.reshape(1, HD), s0)


def _s5disc_kernel(lre_ref, lim_ref, ls_ref, bre_ref, bim_ref, lam_ref, bb_ref):
    lam_re = lre_ref[...]
    lam_im = lim_ref[...]
    step = jnp.exp(ls_ref[...])
    mag = jnp.exp(lam_re * step)
    ang = lam_im * step
    lb_re = mag * jnp.cos(ang)
    lb_im = mag * jnp.sin(ang)
    nr, ni = lb_re - 1.0, lb_im
    den = lam_re * lam_re + lam_im * lam_im
    f_re = (nr * lam_re + ni * lam_im) / den
    f_im = (ni * lam_re - nr * lam_im) / den
    lam_ref[:, 0, :] = lb_re
    lam_ref[:, 1, :] = lb_im
    b_re = bre_ref[...]
    b_im = bim_ref[...]
    for d in range(2):
        fr = f_re[d:d + 1]
        fi = f_im[d:d + 1]
        bb_ref[d, 0] = fr * b_re - fi * b_im
        bb_ref[d, 1] = fr * b_im + fi * b_re


def _s5_discretise(lam_re, lam_im, log_step, b_re, b_im):
    ls = jnp.repeat(log_step, SSM_P, axis=1)
    bt = lambda b: b.reshape(SSM_N, SSM_GC).T
    return pl.pallas_call(
        _s5disc_kernel,
        out_shape=[jax.ShapeDtypeStruct((2, 2, SSM_N), F32),
                   jax.ShapeDtypeStruct((2, 2, SSM_GC, SSM_N), F32)],
        name="s5_discretise",
    )(lam_re.reshape(2, SSM_N), lam_im.reshape(2, SSM_N), ls, bt(b_re), bt(b_im))


def _s5_block_mats(bb, c_re, c_im):
    eye = jnp.eye(SUB, dtype=F32)
    b6 = bb.reshape(2, 2, SSM_GC, SSM_BLK, SUB, SSM_P)
    bd = jnp.einsum("drcjgp,gh->djgcrhp", b6, eye).reshape(2, SSM_BLK, SUB * SSM_GC, 2 * SUB * SSM_P)
    c2 = jnp.stack([c_re, -c_im], axis=0).reshape(2, SSM_BLK, SUB, SSM_GC, SSM_P)
    cm = jnp.einsum("rjgcp,gh->jrgphc", c2, eye).reshape(SSM_BLK, 2 * SUB * SSM_P, SUB * SSM_GC)
    return bd.astype(BF16), cm.astype(BF16)


def _s5_kernel(uf_ref, ub_ref, bd_ref, cm_ref, lam_ref, h0_ref, yf_ref, yb_ref, fin_ref,
               xf_s, xb_s, car_s):
    i = pl.program_id(1)
    tl = uf_ref.shape[0]
    rows = tl * SUB
    bw = 2 * SUB * SSM_P
    half = SUB * SSM_P

    @pl.when(i == 0)
    def _():
        car_s[...] = h0_ref[...]

    uf = uf_ref[...].reshape(rows, SSM_W).astype(BF16)
    ub = ub_ref[...].reshape(rows, SSM_W).astype(BF16)
    for j in range(SSM_BLK):
        cin = slice(j * SUB * SSM_GC, (j + 1) * SUB * SSM_GC)
        xf_s[:, j * bw:(j + 1) * bw] = jnp.dot(uf[:, cin], bd_ref[0, j], preferred_element_type=F32)
        xb_s[:, j * bw:(j + 1) * bw] = jnp.dot(ub[:, cin], bd_ref[1, j], preferred_element_type=F32)

    lw = 256
    for j in range(SSM_BLK):
        for kk in range(half // lw):
            cre = slice(j * bw + kk * lw, j * bw + (kk + 1) * lw)
            cim = slice(j * bw + half + kk * lw, j * bw + half + (kk + 1) * lw)
            n0 = j * half + kk * lw
            lam = [(jnp.broadcast_to(lam_ref[d, 0:1, n0:n0 + lw], (SUB, lw)),
                    jnp.broadcast_to(lam_ref[d, 1:2, n0:n0 + lw], (SUB, lw))) for d in range(2)]

            def step(t, carry, cre=cre, cim=cim, lam=lam):
                fr, fi, br, bi = carry
                rf = pl.ds(pl.multiple_of(t * SUB, SUB), SUB)
                rb = pl.ds(pl.multiple_of((tl - 1 - t) * SUB, SUB), SUB)
                (lr, li), (mr, mi) = lam
                nfr = lr * fr - li * fi + xf_s[rf, cre]
                nfi = lr * fi + li * fr + xf_s[rf, cim]
                nbr = mr * br - mi * bi + xb_s[rb, cre]
                nbi = mr * bi + mi * br + xb_s[rb, cim]
                xf_s[rf, cre] = nfr
                xf_s[rf, cim] = nfi
                xb_s[rb, cre] = nbr
                xb_s[rb, cim] = nbi
                return nfr, nfi, nbr, nbi

            init = (car_s[0, :, cre], car_s[0, :, cim], car_s[1, :, cre], car_s[1, :, cim])
            fr, fi, br, bi = lax.fori_loop(0, tl, step, init, unroll=4)
            car_s[0, :, cre] = fr
            car_s[0, :, cim] = fi
            car_s[1, :, cre] = br
            car_s[1, :, cim] = bi

    for j in range(SSM_BLK):
        cout = slice(j * SUB * SSM_GC, (j + 1) * SUB * SSM_GC)
        yf = jnp.dot(xf_s[:, j * bw:(j + 1) * bw].astype(BF16), cm_ref[j], preferred_element_type=F32)
        yb = jnp.dot(xb_s[:, j * bw:(j + 1) * bw].astype(BF16), cm_ref[j], preferred_element_type=F32)
        yf_ref[:, :, cout] = yf.reshape(tl, SUB, SUB * SSM_GC)
        yb_ref[:, :, cout] = yb.reshape(tl, SUB, SUB * SSM_GC)

    @pl.when(i == pl.num_programs(1) - 1)
    def _():
        fin_ref[...] = car_s[...]


def _s5_scan(u_tm, bd, cm, lam, h0):
    n_grp, L = u_tm.shape[0], u_tm.shape[1]
    tl = 32
    nt = L // tl
    ncol = 2 * SSM_N
    fwd = pl.BlockSpec((None, tl, SUB, SSM_W), lambda g, i: (g, i, 0, 0))
    bwd = pl.BlockSpec((None, tl, SUB, SSM_W), lambda g, i: (g, nt - 1 - i, 0, 0))
    st = pl.BlockSpec((None, 2, SUB, ncol), lambda g, i: (g, 0, 0, 0))
    return pl.pallas_call(
        _s5_kernel,
        grid=(n_grp, nt),
        in_specs=[fwd, bwd, _const_spec(bd.shape), _const_spec(cm.shape), _const_spec(lam.shape), st],
        out_specs=[fwd, bwd, st],
        out_shape=[jax.ShapeDtypeStruct(u_tm.shape, F32), jax.ShapeDtypeStruct(u_tm.shape, F32),
                   jax.ShapeDtypeStruct((n_grp, 2, SUB, ncol), F32)],
        scratch_shapes=[pltpu.VMEM((tl * SUB, ncol), F32), pltpu.VMEM((tl * SUB, ncol), F32),
                        pltpu.VMEM((2, SUB, ncol), F32)],
        compiler_params=_cparams("arbitrary", "arbitrary"),
        name="s5_scan",
    )(u_tm, u_tm, bd, cm, lam, h0)


def _state_to_cols(h_re, h_im):
    b = h_re.shape[0]
    r = h_re.reshape(b, 2, SSM_BLK, SUB * SSM_P)
    m = h_im.reshape(b, 2, SSM_BLK, SUB * SSM_P)
    return jnp.stack([r, m], axis=3).reshape(b, 2, 2 * SSM_N)


def _cols_to_state(cols):
    b = cols.shape[0]
    c = cols.reshape(b, 2, SSM_BLK, 2, SUB * SSM_P)
    return (c[:, :, :, 0].reshape(b, 2, SSM_G, SSM_P), c[:, :, :, 1].reshape(b, 2, SSM_G, SSM_P))


def _pool_kernel(sp_ref, o_ref, pad_s):
    L = o_ref.shape[0]
    pad_s[0:PAD, :] = jnp.zeros((PAD, POOL_W), F32)
    pad_s[PAD + L:PAD + L + PAD, :] = jnp.zeros((PAD, POOL_W), F32)
    pad_s[PAD:PAD + L, :] = sp_ref[...]
    rc = 256
    for gi, w in enumerate(POOL_WINDOWS):
        cols = slice(gi * POOL_G, (gi + 1) * POOL_G)
        for r0 in range(0, L, rc):
            acc = pad_s[PAD + r0 - w // 2:PAD + r0 - w // 2 + rc, cols]
            for dlt in range(-(w // 2) + 1, w // 2):
                acc = acc + pad_s[PAD + r0 + dlt:PAD + r0 + dlt + rc, cols]
            t = r0 + lax.broadcasted_iota(jnp.int32, (rc, POOL_G), 0)
            cnt = jnp.minimum(t + w // 2, L) - jnp.maximum(t - w // 2, 0)
            o_ref[r0:r0 + rc, cols] = acc / cnt.astype(F32) - pad_s[PAD + r0:PAD + r0 + rc, cols]


def _pool(sp, n_seq, seq_len):
    L = seq_len
    return pl.pallas_call(
        _pool_kernel,
        grid=(n_seq,),
        in_specs=[pl.BlockSpec((L, POOL_W), lambda i: (i, 1))],
        out_specs=pl.BlockSpec((L, POOL_W), lambda i: (i, 0)),
        out_shape=jax.ShapeDtypeStruct((n_seq * L, POOL_W), F32),
        scratch_shapes=[pltpu.VMEM((L + 2 * PAD, POOL_W), F32)],
        compiler_params=_cparams("arbitrary"),
        name="pool",
    )(sp)


def _merge_kernel(x_ref, odn_ref, yf_ref, yb_ref, sp_ref, pooled_ref, gate_ref, mod_ref,
                  sd_ref, gw_ref, gb_ref, pw_ref, ps_ref, wdn_ref, wssm_ref, wpool_ref, wout_ref, o_ref):
    y = yf_ref[...] + yb_ref[...] + sd_ref[...] * sp_ref[...]
    y = jax.nn.gelu(y)
    o_ssm = y * jax.nn.sigmoid(_mm(y, gw_ref[...]) + gb_ref[...])
    o_pool = jnp.concatenate(
        [_mm(pooled_ref[:, gi * POOL_G:(gi + 1) * POOL_G], pw_ref[gi]) for gi in range(len(POOL_WINDOWS))],
        axis=1) * ps_ref[...]
    merged = (jax.nn.sigmoid(gate_ref[:, 0:D]) * _mm(odn_ref[...], wdn_ref[...])
              + jax.nn.sigmoid(gate_ref[:, D:2 * D]) * _mm(o_ssm, wssm_ref[...])
              + jax.nn.sigmoid(gate_ref[:, 2 * D:3 * D]) * _mm(o_pool, wpool_ref[...]))
    o_ref[...] = x_ref[...] + mod_ref[0, 2:3, :] * _mm(merged, wout_ref[...])


def _merge(x2, odn, yf, yb, sp, pooled, gates, mod, seq_len, ssm_d, glu_w, glu_b, pool_w, pool_scale,
           wdn, wssm, wpool, wout):
    t = x2.shape[0]
    tm = 256
    per_seq = mod.shape[0] > 1
    tiles_per_seq = seq_len // tm
    mod_map = (lambda i: (i // tiles_per_seq, 0, 0)) if per_seq else (lambda i: (0, 0, 0))
    row = lambda w: pl.BlockSpec((tm, w), lambda i: (i, 0))
    return pl.pallas_call(
        _merge_kernel,
        grid=(t // tm,),
        in_specs=[row(D), row(DNW), row(SSM_W), row(SSM_W), row(SSM_W), row(POOL_W), row(3 * D),
                  pl.BlockSpec((1, 6, D), mod_map),
                  _const_spec((1, SSM_W)), _const_spec(glu_w.shape), _const_spec((1, SSM_W)),
                  _const_spec(pool_w.shape), _const_spec((1, POOL_W)),
                  _const_spec(wdn.shape), _const_spec(wssm.shape), _const_spec(wpool.shape),
                  _const_spec(wout.shape)],
        out_specs=row(D),
        out_shape=jax.ShapeDtypeStruct((t, D), F32),
        compiler_params=_cparams("arbitrary"),
        name="merge",
    )(x2, odn, yf, yb, sp, pooled, gates, mod, ssm_d.reshape(1, SSM_W), glu_w, glu_b.reshape(1, SSM_W),
      pool_w, pool_scale.reshape(1, POOL_W), wdn, wssm, wpool, wout)


def _ffn_kernel(n_seq, final, x_ref, mod_ref, g_ref, wup_ref, cw_ref, wdn_ref, fg_ref, o_ref,
                h_s, pad_s, act_s, acc_s):
    rows = x_ref.shape[0]
    L = rows // n_seq
    tm = 256

    def norm_tile(it, carry):
        rs = pl.ds(pl.multiple_of(it * tm, tm), tm)
        y = _rms(x_ref[rs, :]) * g_ref[...]
        h_s[rs, :] = (y * (1.0 + mod_ref[0, 4:5, :]) + mod_ref[0, 3:4, :]).astype(BF16)
        return carry

    lax.fori_loop(0, rows // tm, norm_tile, 0)

    for n in range(n_seq):
        pad_s[n, 0:PAD, :] = jnp.zeros((PAD, 2 * FF_BLK), F32)
        pad_s[n, PAD + L:PAD + L + PAD, :] = jnp.zeros((PAD, 2 * FF_BLK), F32)
    acc_s[...] = jnp.zeros(acc_s.shape, F32)
    rc = min(L, 256)

    def col_block(j, carry):
        c0 = pl.multiple_of(j * FF_BLK, FF_BLK)
        gcols = pl.ds(c0, FF_BLK)
        vcols = pl.ds(pl.multiple_of(DFF + j * FF_BLK, 128), FF_BLK)
        h = h_s[...]
        up_g = jnp.dot(h, wup_ref[:, gcols], preferred_element_type=F32)
        up_v = jnp.dot(h, wup_ref[:, vcols], preferred_element_type=F32)
        wg = cw_ref[:, gcols]
        wv = cw_ref[:, vcols]
        for n in range(n_seq):
            pad_s[n, PAD:PAD + L, 0:FF_BLK] = up_g[n * L:(n + 1) * L]
            pad_s[n, PAD:PAD + L, FF_BLK:2 * FF_BLK] = up_v[n * L:(n + 1) * L]
        for n in range(n_seq):
            for r0 in range(0, L, rc):
                def conv(w, cols):
                    return (w[0:1] * pad_s[n, PAD - 1 + r0:PAD - 1 + r0 + rc, cols]
                            + w[1:2] * pad_s[n, PAD + r0:PAD + r0 + rc, cols]
                            + w[2:3] * pad_s[n, PAD + 1 + r0:PAD + 1 + r0 + rc, cols])
                hg = conv(wg, slice(0, FF_BLK))
                hv = conv(wv, slice(FF_BLK, 2 * FF_BLK))
                act_s[n * L + r0:n * L + r0 + rc, :] = (_silu(hg) * hv).astype(BF16)
        acc_s[...] += jnp.dot(act_s[...], wdn_ref[pl.ds(c0, FF_BLK), :], preferred_element_type=F32)
        return carry

    lax.fori_loop(0, DFF // FF_BLK, col_block, 0)

    def out_tile(it, carry):
        rs = pl.ds(pl.multiple_of(it * tm, tm), tm)
        y = x_ref[rs, :] + mod_ref[0, 5:6, :] * acc_s[rs, :]
        if final:
            y = _rms(y) * fg_ref[...]
        o_ref[rs, :] = y
        return carry

    lax.fori_loop(0, rows // tm, out_tile, 0)


def _ffn(x2, mod, n_tile_seq, seq_len, norm_g, wup, conv_w, wdn, final_g, final):
    t = x2.shape[0]
    rows = n_tile_seq * seq_len
    per_seq = mod.shape[0] > 1
    mod_map = (lambda i: (i, 0, 0)) if per_seq else (lambda i: (0, 0, 0))
    if per_seq:
        assert n_tile_seq == 1
    row = pl.BlockSpec((rows, D), lambda i: (i, 0))
    return pl.pallas_call(
        functools.partial(_ffn_kernel, n_tile_seq, final),
        grid=(t // rows,),
        in_specs=[row, pl.BlockSpec((1, 6, D), mod_map), _const_spec((1, D)), _const_spec(wup.shape),
                  _const_spec(conv_w.shape), _const_spec(wdn.shape), _const_spec((1, D))],
        out_specs=row,
        out_shape=jax.ShapeDtypeStruct((t, D), F32),
        scratch_shapes=[pltpu.VMEM((rows, D), BF16),
                        pltpu.VMEM((n_tile_seq, seq_len + 2 * PAD, 2 * FF_BLK), F32),
                        pltpu.VMEM((rows, FF_BLK), BF16),
                        pltpu.VMEM((rows, D), F32)],
        compiler_params=_cparams("arbitrary"),
        name="conv_ffn",
    )(x2, mod, norm_g.reshape(1, D), wup, conv_w, wdn, final_g.reshape(1, D))


def _grid_pos_embed(rows, dim):
    quarter = dim // 4
    omega = 1.0 / (10000.0 ** (jnp.arange(quarter, dtype=F32) / quarter))
    r = jnp.broadcast_to(jnp.arange(rows, dtype=F32)[:, None], (rows, GRID_W)).reshape(-1)
    col = jnp.broadcast_to(jnp.arange(GRID_W, dtype=F32)[None, :], (rows, GRID_W)).reshape(-1)

    def sincos(p):
        ang = p[:, None] * omega[None, :]
        return jnp.concatenate([jnp.sin(ang), jnp.cos(ang)], axis=-1)

    return jnp.concatenate([sincos(r), sincos(col)], axis=-1)


def _to_time_major(u2, n_seq, seq_len):
    w = u2.shape[-1]
    return u2.reshape(n_seq // SUB, SUB, seq_len, w).transpose(0, 2, 1, 3)


def _from_time_major(y, n_seq, seq_len):
    w = y.shape[-1]
    return y.transpose(0, 2, 1, 3).reshape(n_seq * seq_len, w)


def _layer(x2, mod, n_seq, seq_len, n_tile_seq, s_dn, h_cols, p, final):
    qkvz, ab, sp, gates = _in_proj(x2, mod, seq_len, p["norm1_g"], p["wm"], p["wab"], p["wsp"], p["wg"])
    odn, st_dn = _deltanet(qkvz, ab, n_seq, seq_len, p["dn_conv"], p["dn_a_log"], p["dn_dt_bias"],
                           p["dn_norm_g"], s_dn)
    u_tm = _to_time_major(sp[:, :SSM_W], n_seq, seq_len)
    h0 = h_cols.reshape(n_seq // SUB, SUB, 2, 2 * SSM_N).transpose(0, 2, 1, 3)
    yf, yb, fin = _s5_scan(u_tm, p["bd"], p["cm"], p["lam"], h0)
    yf = _from_time_major(yf, n_seq, seq_len)
    yb = _from_time_major(yb, n_seq, seq_len)
    fin = fin.transpose(0, 2, 1, 3).reshape(n_seq, 2, 2 * SSM_N)
    pooled = _pool(sp, n_seq, seq_len)
    x2 = _merge(x2, odn, yf, yb, sp, pooled, gates, mod, seq_len, p["ssm_d"], p["glu_w"], p["ssm_glu_b"],
                p["pool_w"], p["pool_scale"], p["wdn"], p["wssm"], p["wpool"], p["wout"])
    x2 = _ffn(x2, mod, n_tile_seq, seq_len, p["norm2_g"], p["wup"], p["ffn_conv"], p["wdown"],
              p["final_norm_g"], final)
    return x2, st_dn, fin


def kernel(x_prompt, x_sample, state_dn, state_ssm_re, state_ssm_im, c, c_ctx, norm1_g, norm2_g, w_ada, b_ada, w_in, dn_conv, dn_a_log, dn_dt_bias, dn_norm_g, ssm_lambda_re, ssm_lambda_im, ssm_log_step, ssm_b_re, ssm_b_im, ssm_c_re, ssm_c_im, ssm_d, ssm_glu_w, ssm_glu_b, pool_w, pool_scale, w_branch_dn, w_branch_ssm, w_branch_pool, w_out, ffn_w_up, ffn_conv, ffn_w_down, final_norm_g):
    bc, lc, _ = x_prompt.shape
    bl, ll, _ = x_sample.shape

    n_cond = 16
    conds = jnp.zeros((n_cond, D), F32).at[0].set(c_ctx).at[1:1 + bl].set(c)
    mods = _ada(conds, w_ada, b_ada).reshape(DEPTH, n_cond, 6, D)

    x_ctx = x_prompt.reshape(bc * lc, D)
    x_lat = _add_pos(x_sample, _grid_pos_embed(ll // GRID_W, D)).reshape(bl * ll, D)

    zero_dn = jnp.zeros((bc, 2, HEADS, HD, HD), F32)
    zero_h = jnp.zeros((bc, 2, 2 * SSM_N), F32)
    new_dn, new_re, new_im = [], [], []
    for l in range(DEPTH):
        wi = w_in[l]
        lam, bb = _s5_discretise(ssm_lambda_re[l], ssm_lambda_im[l], ssm_log_step[l], ssm_b_re[l], ssm_b_im[l])
        bd, cm = _s5_block_mats(bb, ssm_c_re[l], ssm_c_im[l])
        wab = jnp.zeros((D, 128), F32).at[:, :4 * HEADS].set(wi[:, 4 * DNW:4 * DNW + 4 * HEADS])
        o_sp = 4 * DNW + 4 * HEADS
        o_g = o_sp + SSM_W + POOL_W
        p = dict(
            norm1_g=norm1_g[l], norm2_g=norm2_g[l], final_norm_g=final_norm_g,
            wm=wi[:, :4 * DNW].astype(BF16), wab=wab.astype(BF16),
            wsp=wi[:, o_sp:o_g].astype(BF16), wg=wi[:, o_g:].astype(BF16),
            dn_conv=dn_conv[l], dn_a_log=dn_a_log[l], dn_dt_bias=dn_dt_bias[l], dn_norm_g=dn_norm_g[l],
            bd=bd, cm=cm, lam=lam, ssm_d=ssm_d[l], glu_w=ssm_glu_w[l].astype(BF16), ssm_glu_b=ssm_glu_b[l],
            pool_w=pool_w[l].astype(BF16), pool_scale=pool_scale[l],
            wdn=w_branch_dn[l].astype(BF16), wssm=w_branch_ssm[l].astype(BF16),
            wpool=w_branch_pool[l].astype(BF16), wout=w_out[l].astype(BF16),
            wup=ffn_w_up[l].astype(BF16), ffn_conv=ffn_conv[l], wdown=ffn_w_down[l].astype(BF16),
        )
        final = l == DEPTH - 1
        x_ctx, st_dn, fin = _layer(x_ctx, mods[l, 0:1], bc, lc, 4, zero_dn, zero_h, p, final)
        new_dn.append(st_dn)
        f_re, f_im = _cols_to_state(fin)
        new_re.append(f_re)
        new_im.append(f_im)
        h_lat = _state_to_cols(state_ssm_re[:, l], state_ssm_im[:, l])
        x_lat, _, _ = _layer(x_lat, mods[l, 1:1 + bl], bl, ll, 1, state_dn[:, l], h_lat, p, final)

    return (x_ctx.reshape(bc, lc, D), x_lat.reshape(bl, ll, D), jnp.stack(new_dn, axis=1),
            jnp.stack(new_re, axis=1), jnp.stack(new_im, axis=1))
```

```python
import functools

import jax
import jax.numpy as jnp
from jax import lax
from jax.experimental import pallas as pl
from jax.experimental.pallas import tpu as pltpu

F32 = jnp.float32
BF16 = jnp.bfloat16

D = 1024
DEPTH = 2
GRID_W = 64
EPS = 1e-6
HEADS = 4
HD = 128
DNW = HEADS * HD
CH = 128
N_LVL = 7
SSM_W = 512
SSM_G = 32
SSM_GC = 16
SSM_P = 64
SSM_N = SSM_G * SSM_P
SSM_BLK = 4
POOL_WINDOWS = (2, 4, 8, 16)
POOL_W = 512
POOL_G = 128
DFF = 2816
FF_BLK = 256
PAD = 8
SUB = 8
VMEM_LIMIT = 56 << 20


def _cparams(*sem):
    return pltpu.CompilerParams(dimension_semantics=sem, vmem_limit_bytes=VMEM_LIMIT)


def _const_spec(shape):
    zeros = (0,) * len(shape)
    return pl.BlockSpec(shape, lambda *_: zeros, pipeline_mode=pl.Buffered(1))


def _mm(a, b):
    return jnp.dot(a.astype(BF16), b.astype(BF16), preferred_element_type=F32)


def _mm_nt(a, b):
    return lax.dot_general(a.astype(BF16), b.astype(BF16), (((1,), (1,)), ((), ())),
                           preferred_element_type=F32)


def _mm_tn(a, b):
    return lax.dot_general(a.astype(BF16), b.astype(BF16), (((0,), (0,)), ((), ())),
                           preferred_element_type=F32)


def _mm_exact(a, b):
    return jnp.dot(a, b, precision=lax.Precision.HIGHEST, preferred_element_type=F32)


def _silu(x):
    return x * jax.nn.sigmoid(x)


def _rms(x):
    return x * lax.rsqrt(jnp.mean(x * x, axis=-1, keepdims=True) + EPS)


def _ada_kernel(c_ref, w_ref, b_ref, o_ref):
    o_ref[...] = _mm(_silu(c_ref[...]), w_ref[...]) + b_ref[...]


def _ada(conds, w_ada, b_ada):
    n = conds.shape[0]
    tn = 1536
    return pl.pallas_call(
        _ada_kernel,
        grid=(DEPTH, 6 * D // tn),
        in_specs=[pl.BlockSpec((n, D), lambda l, j: (0, 0)),
                  pl.BlockSpec((None, D, tn), lambda l, j: (l, 0, j)),
                  pl.BlockSpec((None, 1, tn), lambda l, j: (l, 0, j))],
        out_specs=pl.BlockSpec((None, n, tn), lambda l, j: (l, 0, j)),
        out_shape=jax.ShapeDtypeStruct((DEPTH, n, 6 * D), F32),
        compiler_params=_cparams("arbitrary", "arbitrary"),
        name="ada",
    )(conds, w_ada, b_ada.reshape(DEPTH, 1, 6 * D))


def _addpos_kernel(x_ref, p_ref, o_ref):
    o_ref[...] = x_ref[...] + p_ref[...]


def _add_pos(x, pos):
    b, l, _ = x.shape
    return pl.pallas_call(
        _addpos_kernel,
        grid=(b,),
        in_specs=[pl.BlockSpec((None, l, D), lambda i: (i, 0, 0)), _const_spec((l, D))],
        out_specs=pl.BlockSpec((None, l, D), lambda i: (i, 0, 0)),
        out_shape=jax.ShapeDtypeStruct(x.shape, F32),
        compiler_params=_cparams("arbitrary"),
        name="add_pos",
    )(x, pos)


def _inproj_kernel(x_ref, mod_ref, g_ref, wm_ref, wab_ref, wsp_ref, wg_ref,
                   qkvz_ref, ab_ref, sp_ref, gate_ref):
    y = _rms(x_ref[...]) * g_ref[...]
    h = (y * (1.0 + mod_ref[0, 1:2, :]) + mod_ref[0, 0:1, :]).astype(BF16)
    qkvz_ref[...] = jnp.dot(h, wm_ref[...], preferred_element_type=F32)
    ab_ref[...] = jnp.dot(h, wab_ref[...], preferred_element_type=F32)
    sp_ref[...] = jnp.dot(h, wsp_ref[...], preferred_element_type=F32)
    gate_ref[...] = jnp.dot(h, wg_ref[...], preferred_element_type=F32)


def _in_proj(x2, mod, seq_len, norm_g, wm, wab, wsp, wg):
    t = x2.shape[0]
    tm = 512
    per_seq = mod.shape[0] > 1
    tiles_per_seq = seq_len // tm
    if per_seq:
        assert seq_len % tm == 0
        mod_map = lambda i: (i // tiles_per_seq, 0, 0)
    else:
        mod_map = lambda i: (0, 0, 0)
    row = lambda w: pl.BlockSpec((tm, w), lambda i: (i, 0))
    return pl.pallas_call(
        _inproj_kernel,
        grid=(t // tm,),
        in_specs=[row(D), pl.BlockSpec((1, 6, D), mod_map), _const_spec((1, D)),
                  _const_spec(wm.shape), _const_spec(wab.shape), _const_spec(wsp.shape),
                  _const_spec(wg.shape)],
        out_specs=[row(4 * DNW), row(128), row(SSM_W + POOL_W), row(3 * D)],
        out_shape=[jax.ShapeDtypeStruct((t, 4 * DNW), F32), jax.ShapeDtypeStruct((t, 128), F32),
                   jax.ShapeDtypeStruct((t, SSM_W + POOL_W), F32),
                   jax.ShapeDtypeStruct((t, 3 * D), F32)],
        compiler_params=_cparams("arbitrary"),
        name="in_proj",
    )(x2, mod, norm_g.reshape(1, D), wm, wab, wsp, wg)


def _dn_kernel(qkv_ref, z_ref, ab_ref, cw_ref, alog_ref, dtb_ref, ng_ref, s0_ref,
               o_ref, sfin_ref,
               pad_s, q_s, k_s, v_s, g_s, gt_s, b_s, o_s, st_s, lm_s):
    L = o_ref.shape[0]
    nc = L // CH

    pad_s[0:PAD, :] = jnp.zeros((PAD, 3 * DNW), F32)
    pad_s[PAD + L:PAD + L + PAD, :] = jnp.zeros((PAD, 3 * DNW), F32)
    pad_s[PAD:PAD + L, :] = qkv_ref[...]
    rc = 256
    for cb in range(3 * HEADS):
        cols = slice(cb * HD, (cb + 1) * HD)
        w = cw_ref[:, cols]
        for r0 in range(0, L, rc):
            y = (w[0:1] * pad_s[PAD - 1 + r0:PAD - 1 + r0 + rc, cols]
                 + w[1:2] * pad_s[PAD + r0:PAD + r0 + rc, cols]
                 + w[2:3] * pad_s[PAD + 1 + r0:PAD + 1 + r0 + rc, cols])
            y = _silu(y)
            h = cb % HEADS
            hc = slice(h * HD, (h + 1) * HD)
            if cb < HEADS:
                y = y * lax.rsqrt(jnp.sum(y * y, axis=-1, keepdims=True) + EPS) * (HD ** -0.5)
                q_s[r0:r0 + rc, hc] = y
            elif cb < 2 * HEADS:
                y = y * lax.rsqrt(jnp.sum(y * y, axis=-1, keepdims=True) + EPS)
                k_s[r0:r0 + rc, hc] = y
            else:
                v_s[r0:r0 + rc, hc] = y

    ab = ab_ref[...]
    xs = ab + dtb_ref[...]
    softplus = jnp.maximum(xs, 0.0) + jnp.log1p(jnp.exp(-jnp.abs(xs)))
    g = -jnp.exp(alog_ref[...]) * softplus
    g_s[...] = g
    gt_s[...] = g.T
    b_s[...] = jax.nn.sigmoid(ab)

    for d in range(2):
        for h in range(HEADS):
            st_s[d * HEADS + h] = s0_ref[d, h]
    o_s[...] = jnp.zeros(o_s.shape, F32)

    ri = lax.broadcasted_iota(jnp.int32, (CH, CH), 0)
    ci = lax.broadcasted_iota(jnp.int32, (CH, CH), 1)
    lo = (ri >= ci).astype(F32)
    up = (ri <= ci).astype(F32)
    eye = (ri == ci).astype(F32)
    for lvl in range(N_LVL):
        pair = ((ri >> (lvl + 1)) == (ci >> (lvl + 1))) & ((ri >> lvl) != (ci >> lvl))
        lm_s[lvl] = pair.astype(F32)

    streams = [(d, h) for d in range(2) for h in range(HEADS)]

    def chunk_step(c, carry):
        rows, gcol, grow = [], [], []
        for d in range(2):
            cidx = c if d == 0 else nc - 1 - c
            rows.append(pl.ds(pl.multiple_of(cidx * CH, CH), CH))
            tri, tri_t = (lo, up) if d == 0 else (up, lo)
            gcol.append(_mm_exact(tri, g_s[rows[d], :]))
            grow.append(_mm_exact(gt_s[:, rows[d]], tri_t))
        incl = [ri >= ci, ri <= ci]
        strict = [ri > ci, ri < ci]
        last = [CH - 1, 0]
        hcs = [slice(h * HD, (h + 1) * HD) for h in range(HEADS)]
        q, k, v, gc, gtot, beta, decay, eg, kb, a = ([] for _ in range(10))
        for d, h in streams:
            s = d * HEADS + h
            q.append(q_s[rows[d], hcs[h]])
            k.append(k_s[rows[d], hcs[h]])
            v.append(v_s[rows[d], hcs[h]])
            gc.append(gcol[d][:, s:s + 1])
            gtot.append(gcol[d][last[d]:last[d] + 1, s:s + 1])
            beta.append(b_s[rows[d], 2 * HEADS + s:2 * HEADS + s + 1])
            decay.append(jnp.where(incl[d], jnp.exp(gc[s] - grow[d][s:s + 1, :]), 0.0))
            eg.append(jnp.exp(gc[s]))
            kb.append(k[s] * beta[s])
        for d, h in streams:
            s = d * HEADS + h
            a.append(jnp.where(strict[d], _mm_nt(kb[s], k[s]) * decay[s], 0.0))
        t = [eye - a[s] * lm_s[0] for s in range(len(streams))]
        for lvl in range(1, N_LVL):
            x = [_mm(a[s] * lm_s[lvl], t[s]) for s in range(len(streams))]
            t = [t[s] - _mm(t[s], x[s]) for s in range(len(streams))]
        uw = [_mm(t[s], jnp.concatenate([v[s] * beta[s], kb[s] * eg[s]], axis=1)) for s in range(len(streams))]
        qk = [_mm_nt(q[s], k[s]) * decay[s] for s in range(len(streams))]
        st = [st_s[s] for s in range(len(streams))]
        ws = [_mm(jnp.concatenate([uw[s][:, HD:], q[s] * eg[s]], axis=0), st[s]) for s in range(len(streams))]
        v_new = [uw[s][:, :HD] - ws[s][:CH] for s in range(len(streams))]
        o = [ws[s][CH:] + _mm(qk[s], v_new[s]) for s in range(len(streams))]
        for d, h in streams:
            s = d * HEADS + h
            st_s[s] = st[s] * jnp.exp(gtot[s]) + _mm_tn(k[s] * jnp.exp(gtot[s] - gc[s]), v_new[s])
        for d, h in streams:
            s = d * HEADS + h
            o_s[rows[d], hcs[h]] = o_s[rows[d], hcs[h]] + o[s]
        return carry

    lax.fori_loop(0, nc, chunk_step, 0)

    for d in range(2):
        for h in range(HEADS):
            sfin_ref[d, h] = st_s[d * HEADS + h]
    for h in range(HEADS):
        hc = slice(h * HD, (h + 1) * HD)
        for r0 in range(0, L, rc):
            z = z_ref[r0:r0 + rc, hc]
            o_ref[r0:r0 + rc, hc] = _rms(o_s[r0:r0 + rc, hc]) * ng_ref[...] * _silu(z)


def _deltanet(qkvz, ab, n_seq, seq_len, conv_w, a_log, dt_bias, norm_g, s0):
    L = seq_len
    pad_row = lambda v: jnp.zeros((1, 128), F32).at[0, :2 * HEADS].set(v.reshape(-1))
    st_spec = pl.BlockSpec((None, 2, HEADS, HD, HD), lambda i: (i, 0, 0, 0, 0))
    return pl.pallas_call(
        _dn_kernel,
        grid=(n_seq,),
        in_specs=[pl.BlockSpec((L, 3 * DNW), lambda i: (i, 0)),
                  pl.BlockSpec((L, DNW), lambda i: (i, 3)),
                  pl.BlockSpec((L, 128), lambda i: (i, 0)),
                  _const_spec((3, 3 * DNW)), _const_spec((1, 128)), _const_spec((1, 128)),
                  _const_spec((1, HD)), st_spec],
        out_specs=[pl.BlockSpec((L, DNW), lambda i: (i, 0)), st_spec],
        out_shape=[jax.ShapeDtypeStruct((n_seq * L, DNW), F32),
                   jax.ShapeDtypeStruct((n_seq, 2, HEADS, HD, HD), F32)],
        scratch_shapes=[pltpu.VMEM((L + 2 * PAD, 3 * DNW), F32),
                        pltpu.VMEM((L, DNW), F32), pltpu.VMEM((L, DNW), F32), pltpu.VMEM((L, DNW), F32),
                        pltpu.VMEM((L, 128), F32), pltpu.VMEM((128, L), F32), pltpu.VMEM((L, 128), F32),
                        pltpu.VMEM((L, DNW), F32), pltpu.VMEM((2 * HEADS, HD, HD), F32),
                        pltpu.VMEM((N_LVL, CH, CH), F32)],
        compiler_params=_cparams("arbitrary"),
        name="deltanet",
    )(qkvz, qkvz, ab, conv_w, pad_row(a_log), pad_row(dt_bias), norm_g.reshape(1, HD), s0)


def _s5disc_kernel(lre_ref, lim_ref, ls_ref, bre_ref, bim_ref, lam_ref, bb_ref):
    lam_re = lre_ref[...]
    lam_im = lim_ref[...]
    step = jnp.exp(ls_ref[...])
    mag = jnp.exp(lam_re * step)
    ang = lam_im * step
    lb_re = mag * jnp.cos(ang)
    lb_im = mag * jnp.sin(ang)
    nr, ni = lb_re - 1.0, lb_im
    den = lam_re * lam_re + lam_im * lam_im
    f_re = (nr * lam_re + ni * lam_im) / den
    f_im = (ni * lam_re - nr * lam_im) / den
    lam_ref[:, 0, :] = lb_re
    lam_ref[:, 1, :] = lb_im
    b_re = bre_ref[...]
    b_im = bim_ref[...]
    for d in range(2):
        fr = f_re[d:d + 1]
        fi = f_im[d:d + 1]
        bb_ref[d, 0] = fr * b_re - fi * b_im
        bb_ref[d, 1] = fr * b_im + fi * b_re


def _s5_discretise(lam_re, lam_im, log_step, b_re, b_im):
    ls = jnp.repeat(log_step, SSM_P, axis=1)
    bt = lambda b: b.reshape(SSM_N, SSM_GC).T
    return pl.pallas_call(
        _s5disc_kernel,
        out_shape=[jax.ShapeDtypeStruct((2, 2, SSM_N), F32),
                   jax.ShapeDtypeStruct((2, 2, SSM_GC, SSM_N), F32)],
        name="s5_discretise",
    )(lam_re.reshape(2, SSM_N), lam_im.reshape(2, SSM_N), ls, bt(b_re), bt(b_im))


def _s5_block_mats(bb, c_re, c_im):
    eye = jnp.eye(SUB, dtype=F32)
    b6 = bb.reshape(2, 2, SSM_GC, SSM_BLK, SUB, SSM_P)
    bd = jnp.einsum("drcjgp,gh->djgcrhp", b6, eye).reshape(2, SSM_BLK, SUB * SSM_GC, 2 * SUB * SSM_P)
    c2 = jnp.stack([c_re, -c_im], axis=0).reshape(2, SSM_BLK, SUB, SSM_GC, SSM_P)
    cm = jnp.einsum("rjgcp,gh->jrgphc", c2, eye).reshape(SSM_BLK, 2 * SUB * SSM_P, SUB * SSM_GC)
    return bd.astype(BF16), cm.astype(BF16)


def _s5_kernel(uf_ref, ub_ref, bd_ref, cm_ref, lam_ref, h0_ref, yf_ref, yb_ref, fin_ref,
               xf_s, xb_s, car_s):
    i = pl.program_id(1)
    tl = uf_ref.shape[0]
    rows = tl * SUB
    bw = 2 * SUB * SSM_P
    half = SUB * SSM_P

    @pl.when(i == 0)
    def _():
        car_s[...] = h0_ref[...]

    uf = uf_ref[...].reshape(rows, SSM_W).astype(BF16)
    ub = ub_ref[...].reshape(rows, SSM_W).astype(BF16)
    for j in range(SSM_BLK):
        cin = slice(j * SUB * SSM_GC, (j + 1) * SUB * SSM_GC)
        xf_s[:, j * bw:(j + 1) * bw] = jnp.dot(uf[:, cin], bd_ref[0, j], preferred_element_type=F32)
        xb_s[:, j * bw:(j + 1) * bw] = jnp.dot(ub[:, cin], bd_ref[1, j], preferred_element_type=F32)

    lw = 256
    for j in range(SSM_BLK):
        for kk in range(half // lw):
            cre = slice(j * bw + kk * lw, j * bw + (kk + 1) * lw)
            cim = slice(j * bw + half + kk * lw, j * bw + half + (kk + 1) * lw)
            n0 = j * half + kk * lw
            lam = [(jnp.broadcast_to(lam_ref[d, 0:1, n0:n0 + lw], (SUB, lw)),
                    jnp.broadcast_to(lam_ref[d, 1:2, n0:n0 + lw], (SUB, lw))) for d in range(2)]

            def step(t, carry, cre=cre, cim=cim, lam=lam):
                fr, fi, br, bi = carry
                rf = pl.ds(pl.multiple_of(t * SUB, SUB), SUB)
                rb = pl.ds(pl.multiple_of((tl - 1 - t) * SUB, SUB), SUB)
                (lr, li), (mr, mi) = lam
                nfr = lr * fr - li * fi + xf_s[rf, cre]
                nfi = lr * fi + li * fr + xf_s[rf, cim]
                nbr = mr * br - mi * bi + xb_s[rb, cre]
                nbi = mr * bi + mi * br + xb_s[rb, cim]
                xf_s[rf, cre] = nfr
                xf_s[rf, cim] = nfi
                xb_s[rb, cre] = nbr
                xb_s[rb, cim] = nbi
                return nfr, nfi, nbr, nbi

            init = (car_s[0, :, cre], car_s[0, :, cim], car_s[1, :, cre], car_s[1, :, cim])
            fr, fi, br, bi = lax.fori_loop(0, tl, step, init, unroll=4)
            car_s[0, :, cre] = fr
            car_s[0, :, cim] = fi
            car_s[1, :, cre] = br
            car_s[1, :, cim] = bi

    for j in range(SSM_BLK):
        cout = slice(j * SUB * SSM_GC, (j + 1) * SUB * SSM_GC)
        yf = jnp.dot(xf_s[:, j * bw:(j + 1) * bw].astype(BF16), cm_ref[j], preferred_element_type=F32)
        yb = jnp.dot(xb_s[:, j * bw:(j + 1) * bw].astype(BF16), cm_ref[j], preferred_element_type=F32)
        yf_ref[:, :, cout] = yf.reshape(tl, SUB, SUB * SSM_GC)
        yb_ref[:, :, cout] = yb.reshape(tl, SUB, SUB * SSM_GC)

    @pl.when(i == pl.num_programs(1) - 1)
    def _():
        fin_ref[...] = car_s[...]


def _s5_scan(u_tm, bd, cm, lam, h0):
    n_grp, L = u_tm.shape[0], u_tm.shape[1]
    tl = 32
    nt = L // tl
    ncol = 2 * SSM_N
    fwd = pl.BlockSpec((None, tl, SUB, SSM_W), lambda g, i: (g, i, 0, 0))
    bwd = pl.BlockSpec((None, tl, SUB, SSM_W), lambda g, i: (g, nt - 1 - i, 0, 0))
    st = pl.BlockSpec((None, 2, SUB, ncol), lambda g, i: (g, 0, 0, 0))
    return pl.pallas_call(
        _s5_kernel,
        grid=(n_grp, nt),
        in_specs=[fwd, bwd, _const_spec(bd.shape), _const_spec(cm.shape), _const_spec(lam.shape), st],
        out_specs=[fwd, bwd, st],
        out_shape=[jax.ShapeDtypeStruct(u_tm.shape, F32), jax.ShapeDtypeStruct(u_tm.shape, F32),
                   jax.ShapeDtypeStruct((n_grp, 2, SUB, ncol), F32)],
        scratch_shapes=[pltpu.VMEM((tl * SUB, ncol), F32), pltpu.VMEM((tl * SUB, ncol), F32),
                        pltpu.VMEM((2, SUB, ncol), F32)],
        compiler_params=_cparams("arbitrary", "arbitrary"),
        name="s5_scan",
    )(u_tm, u_tm, bd, cm, lam, h0)


def _state_to_cols(h_re, h_im):
    b = h_re.shape[0]
    r = h_re.reshape(b, 2, SSM_BLK, SUB * SSM_P)
    m = h_im.reshape(b, 2, SSM_BLK, SUB * SSM_P)
    return jnp.stack([r, m], axis=3).reshape(b, 2, 2 * SSM_N)


def _cols_to_state(cols):
    b = cols.shape[0]
    c = cols.reshape(b, 2, SSM_BLK, 2, SUB * SSM_P)
    return (c[:, :, :, 0].reshape(b, 2, SSM_G, SSM_P), c[:, :, :, 1].reshape(b, 2, SSM_G, SSM_P))


def _pool_kernel(sp_ref, o_ref, pad_s):
    L = o_ref.shape[0]
    pad_s[0:PAD, :] = jnp.zeros((PAD, POOL_W), F32)
    pad_s[PAD + L:PAD + L + PAD, :] = jnp.zeros((PAD, POOL_W), F32)
    pad_s[PAD:PAD + L, :] = sp_ref[...]
    rc = 256
    for gi, w in enumerate(POOL_WINDOWS):
        cols = slice(gi * POOL_G, (gi + 1) * POOL_G)
        for r0 in range(0, L, rc):
            acc = pad_s[PAD + r0 - w // 2:PAD + r0 - w // 2 + rc, cols]
            for dlt in range(-(w // 2) + 1, w // 2):
                acc = acc + pad_s[PAD + r0 + dlt:PAD + r0 + dlt + rc, cols]
            t = r0 + lax.broadcasted_iota(jnp.int32, (rc, POOL_G), 0)
            cnt = jnp.minimum(t + w // 2, L) - jnp.maximum(t - w // 2, 0)
            o_ref[r0:r0 + rc, cols] = acc / cnt.astype(F32) - pad_s[PAD + r0:PAD + r0 + rc, cols]


def _pool(sp, n_seq, seq_len):
    L = seq_len
    return pl.pallas_call(
        _pool_kernel,
        grid=(n_seq,),
        in_specs=[pl.BlockSpec((L, POOL_W), lambda i: (i, 1))],
        out_specs=pl.BlockSpec((L, POOL_W), lambda i: (i, 0)),
        out_shape=jax.ShapeDtypeStruct((n_seq * L, POOL_W), F32),
        scratch_shapes=[pltpu.VMEM((L + 2 * PAD, POOL_W), F32)],
        compiler_params=_cparams("arbitrary"),
        name="pool",
    )(sp)


def _merge_kernel(x_ref, odn_ref, yf_ref, yb_ref, sp_ref, pooled_ref, gate_ref, mod_ref,
                  sd_ref, gw_ref, gb_ref, pw_ref, ps_ref, wdn_ref, wssm_ref, wpool_ref, wout_ref, o_ref):
    y = yf_ref[...] + yb_ref[...] + sd_ref[...] * sp_ref[...]
    y = jax.nn.gelu(y)
    o_ssm = y * jax.nn.sigmoid(_mm(y, gw_ref[...]) + gb_ref[...])
    o_pool = jnp.concatenate(
        [_mm(pooled_ref[:, gi * POOL_G:(gi + 1) * POOL_G], pw_ref[gi]) for gi in range(len(POOL_WINDOWS))],
        axis=1) * ps_ref[...]
    merged = (jax.nn.sigmoid(gate_ref[:, 0:D]) * _mm(odn_ref[...], wdn_ref[...])
              + jax.nn.sigmoid(gate_ref[:, D:2 * D]) * _mm(o_ssm, wssm_ref[...])
              + jax.nn.sigmoid(gate_ref[:, 2 * D:3 * D]) * _mm(o_pool, wpool_ref[...]))
    o_ref[...] = x_ref[...] + mod_ref[0, 2:3, :] * _mm(merged, wout_ref[...])


def _merge(x2, odn, yf, yb, sp, pooled, gates, mod, seq_len, ssm_d, glu_w, glu_b, pool_w, pool_scale,
           wdn, wssm, wpool, wout):
    t = x2.shape[0]
    tm = 256
    per_seq = mod.shape[0] > 1
    tiles_per_seq = seq_len // tm
    mod_map = (lambda i: (i // tiles_per_seq, 0, 0)) if per_seq else (lambda i: (0, 0, 0))
    row = lambda w: pl.BlockSpec((tm, w), lambda i: (i, 0))
    return pl.pallas_call(
        _merge_kernel,
        grid=(t // tm,),
        in_specs=[row(D), row(DNW), row(SSM_W), row(SSM_W), row(SSM_W), row(POOL_W), row(3 * D),
                  pl.BlockSpec((1, 6, D), mod_map),
                  _const_spec((1, SSM_W)), _const_spec(glu_w.shape), _const_spec((1, SSM_W)),
                  _const_spec(pool_w.shape), _const_spec((1, POOL_W)),
                  _const_spec(wdn.shape), _const_spec(wssm.shape), _const_spec(wpool.shape),
                  _const_spec(wout.shape)],
        out_specs=row(D),
        out_shape=jax.ShapeDtypeStruct((t, D), F32),
        compiler_params=_cparams("arbitrary"),
        name="merge",
    )(x2, odn, yf, yb, sp, pooled, gates, mod, ssm_d.reshape(1, SSM_W), glu_w, glu_b.reshape(1, SSM_W),
      pool_w, pool_scale.reshape(1, POOL_W), wdn, wssm, wpool, wout)


def _ffn_kernel(n_seq, final, x_ref, mod_ref, g_ref, wup_ref, cw_ref, wdn_ref, fg_ref, o_ref,
                h_s, pad_s, act_s, acc_s):
    rows = x_ref.shape[0]
    L = rows // n_seq
    tm = 256

    def norm_tile(it, carry):
        rs = pl.ds(pl.multiple_of(it * tm, tm), tm)
        y = _rms(x_ref[rs, :]) * g_ref[...]
        h_s[rs, :] = (y * (1.0 + mod_ref[0, 4:5, :]) + mod_ref[0, 3:4, :]).astype(BF16)
        return carry

    lax.fori_loop(0, rows // tm, norm_tile, 0)

    for n in range(n_seq):
        pad_s[n, 0:PAD, :] = jnp.zeros((PAD, 2 * FF_BLK), F32)
        pad_s[n, PAD + L:PAD + L + PAD, :] = jnp.zeros((PAD, 2 * FF_BLK), F32)
    acc_s[...] = jnp.zeros(acc_s.shape, F32)
    rc = min(L, 256)

    def col_block(j, carry):
        c0 = pl.multiple_of(j * FF_BLK, FF_BLK)
        gcols = pl.ds(c0, FF_BLK)
        vcols = pl.ds(pl.multiple_of(DFF + j * FF_BLK, 128), FF_BLK)
        h = h_s[...]
        up_g = jnp.dot(h, wup_ref[:, gcols], preferred_element_type=F32)
        up_v = jnp.dot(h, wup_ref[:, vcols], preferred_element_type=F32)
        wg = cw_ref[:, gcols]
        wv = cw_ref[:, vcols]
        for n in range(n_seq):
            pad_s[n, PAD:PAD + L, 0:FF_BLK] = up_g[n * L:(n + 1) * L]
            pad_s[n, PAD:PAD + L, FF_BLK:2 * FF_BLK] = up_v[n * L:(n + 1) * L]
        for n in range(n_seq):
            for r0 in range(0, L, rc):
                def conv(w, cols):
                    return (w[0:1] * pad_s[n, PAD - 1 + r0:PAD - 1 + r0 + rc, cols]
                            + w[1:2] * pad_s[n, PAD + r0:PAD + r0 + rc, cols]
                            + w[2:3] * pad_s[n, PAD + 1 + r0:PAD + 1 + r0 + rc, cols])
                hg = conv(wg, slice(0, FF_BLK))
                hv = conv(wv, slice(FF_BLK, 2 * FF_BLK))
                act_s[n * L + r0:n * L + r0 + rc, :] = (_silu(hg) * hv).astype(BF16)
        acc_s[...] += jnp.dot(act_s[...], wdn_ref[pl.ds(c0, FF_BLK), :], preferred_element_type=F32)
        return carry

    lax.fori_loop(0, DFF // FF_BLK, col_block, 0)

    def out_tile(it, carry):
        rs = pl.ds(pl.multiple_of(it * tm, tm), tm)
        y = x_ref[rs, :] + mod_ref[0, 5:6, :] * acc_s[rs, :]
        if final:
            y = _rms(y) * fg_ref[...]
        o_ref[rs, :] = y
        return carry

    lax.fori_loop(0, rows // tm, out_tile, 0)


def _ffn(x2, mod, n_tile_seq, seq_len, norm_g, wup, conv_w, wdn, final_g, final):
    t = x2.shape[0]
    rows = n_tile_seq * seq_len
    per_seq = mod.shape[0] > 1
    mod_map = (lambda i: (i, 0, 0)) if per_seq else (lambda i: (0, 0, 0))
    if per_seq:
        assert n_tile_seq == 1
    row = pl.BlockSpec((rows, D), lambda i: (i, 0))
    return pl.pallas_call(
        functools.partial(_ffn_kernel, n_tile_seq, final),
        grid=(t // rows,),
        in_specs=[row, pl.BlockSpec((1, 6, D), mod_map), _const_spec((1, D)), _const_spec(wup.shape),
                  _const_spec(conv_w.shape), _const_spec(wdn.shape), _const_spec((1, D))],
        out_specs=row,
        out_shape=jax.ShapeDtypeStruct((t, D), F32),
        scratch_shapes=[pltpu.VMEM((rows, D), BF16),
                        pltpu.VMEM((n_tile_seq, seq_len + 2 * PAD, 2 * FF_BLK), F32),
                        pltpu.VMEM((rows, FF_BLK), BF16),
                        pltpu.VMEM((rows, D), F32)],
        compiler_params=_cparams("arbitrary"),
        name="conv_ffn",
    )(x2, mod, norm_g.reshape(1, D), wup, conv_w, wdn, final_g.reshape(1, D))


def _grid_pos_embed(rows, dim):
    quarter = dim // 4
    omega = 1.0 / (10000.0 ** (jnp.arange(quarter, dtype=F32) / quarter))
    r = jnp.broadcast_to(jnp.arange(rows, dtype=F32)[:, None], (rows, GRID_W)).reshape(-1)
    col = jnp.broadcast_to(jnp.arange(GRID_W, dtype=F32)[None, :], (rows, GRID_W)).reshape(-1)

    def sincos(p):
        ang = p[:, None] * omega[None, :]
        return jnp.concatenate([jnp.sin(ang), jnp.cos(ang)], axis=-1)

    return jnp.concatenate([sincos(r), sincos(col)], axis=-1)


def _to_time_major(u2, n_seq, seq_len):
    w = u2.shape[-1]
    return u2.reshape(n_seq // SUB, SUB, seq_len, w).transpose(0, 2, 1, 3)


def _from_time_major(y, n_seq, seq_len):
    w = y.shape[-1]
    return y.transpose(0, 2, 1, 3).reshape(n_seq * seq_len, w)


def _layer(x2, mod, n_seq, seq_len, n_tile_seq, s_dn, h_cols, p, final):
    qkvz, ab, sp, gates = _in_proj(x2, mod, seq_len, p["norm1_g"], p["wm"], p["wab"], p["wsp"], p["wg"])
    odn, st_dn = _deltanet(qkvz, ab, n_seq, seq_len, p["dn_conv"], p["dn_a_log"], p["dn_dt_bias"],
                           p["dn_norm_g"], s_dn)
    u_tm = _to_time_major(sp[:, :SSM_W], n_seq, seq_len)
    h0 = h_cols.reshape(n_seq // SUB, SUB, 2, 2 * SSM_N).transpose(0, 2, 1, 3)
    yf, yb, fin = _s5_scan(u_tm, p["bd"], p["cm"], p["lam"], h0)
    yf = _from_time_major(yf, n_seq, seq_len)
    yb = _from_time_major(yb, n_seq, seq_len)
    fin = fin.transpose(0, 2, 1, 3).reshape(n_seq, 2, 2 * SSM_N)
    pooled = _pool(sp, n_seq, seq_len)
    x2 = _merge(x2, odn, yf, yb, sp, pooled, gates, mod, seq_len, p["ssm_d"], p["glu_w"], p["ssm_glu_b"],
                p["pool_w"], p["pool_scale"], p["wdn"], p["wssm"], p["wpool"], p["wout"])
    x2 = _ffn(x2, mod, n_tile_seq, seq_len, p["norm2_g"], p["wup"], p["ffn_conv"], p["wdown"],
              p["final_norm_g"], final)
    return x2, st_dn, fin


def kernel(x_prompt, x_sample, state_dn, state_ssm_re, state_ssm_im, c, c_ctx, norm1_g, norm2_g, w_ada, b_ada, w_in, dn_conv, dn_a_log, dn_dt_bias, dn_norm_g, ssm_lambda_re, ssm_lambda_im, ssm_log_step, ssm_b_re, ssm_b_im, ssm_c_re, ssm_c_im, ssm_d, ssm_glu_w, ssm_glu_b, pool_w, pool_scale, w_branch_dn, w_branch_ssm, w_branch_pool, w_out, ffn_w_up, ffn_conv, ffn_w_down, final_norm_g):
    bc, lc, _ = x_prompt.shape
    bl, ll, _ = x_sample.shape

    n_cond = 16
    conds = jnp.zeros((n_cond, D), F32).at[0].set(c_ctx).at[1:1 + bl].set(c)
    mods = _ada(conds, w_ada, b_ada).reshape(DEPTH, n_cond, 6, D)

    x_ctx = x_prompt.reshape(bc * lc, D)
    x_lat = _add_pos(x_sample, _grid_pos_embed(ll // GRID_W, D)).reshape(bl * ll, D)

    zero_dn = jnp.zeros((bc, 2, HEADS, HD, HD), F32)
    zero_h = jnp.zeros((bc, 2, 2 * SSM_N), F32)
    new_dn, new_re, new_im = [], [], []
    for l in range(DEPTH):
        wi = w_in[l]
        lam, bb = _s5_discretise(ssm_lambda_re[l], ssm_lambda_im[l], ssm_log_step[l], ssm_b_re[l], ssm_b_im[l])
        bd, cm = _s5_block_mats(bb, ssm_c_re[l], ssm_c_im[l])
        wab = jnp.zeros((D, 128), F32).at[:, :4 * HEADS].set(wi[:, 4 * DNW:4 * DNW + 4 * HEADS])
        o_sp = 4 * DNW + 4 * HEADS
        o_g = o_sp + SSM_W + POOL_W
        p = dict(
            norm1_g=norm1_g[l], norm2_g=norm2_g[l], final_norm_g=final_norm_g,
            wm=wi[:, :4 * DNW].astype(BF16), wab=wab.astype(BF16),
            wsp=wi[:, o_sp:o_g].astype(BF16), wg=wi[:, o_g:].astype(BF16),
            dn_conv=dn_conv[l], dn_a_log=dn_a_log[l], dn_dt_bias=dn_dt_bias[l], dn_norm_g=dn_norm_g[l],
            bd=bd, cm=cm, lam=lam, ssm_d=ssm_d[l], glu_w=ssm_glu_w[l].astype(BF16), ssm_glu_b=ssm_glu_b[l],
            pool_w=pool_w[l].astype(BF16), pool_scale=pool_scale[l],
            wdn=w_branch_dn[l].astype(BF16), wssm=w_branch_ssm[l].astype(BF16),
            wpool=w_branch_pool[l].astype(BF16), wout=w_out[l].astype(BF16),
            wup=ffn_w_up[l].astype(BF16), ffn_conv=ffn_conv[l], wdown=ffn_w_down[l].astype(BF16),
        )
        final = l == DEPTH - 1
        x_ctx, st_dn, fin = _layer(x_ctx, mods[l, 0:1], bc, lc, 4, zero_dn, zero_h, p, final)
        new_dn.append(st_dn)
        f_re, f_im = _cols_to_state(fin)
        new_re.append(f_re)
        new_im.append(f_im)
        h_lat = _state_to_cols(state_ssm_re[:, l], state_ssm_im[:, l])
        x_lat, _, _ = _layer(x_lat, mods[l, 1:1 + bl], bl, ll, 1, state_dn[:, l], h_lat, p, final)

    return (x_ctx.reshape(bc, lc, D), x_lat.reshape(bl, ll, D), jnp.stack(new_dn, axis=1),
            jnp.stack(new_re, axis=1), jnp.stack(new_im, axis=1))
```

```python
import functools

import jax
import jax.numpy as jnp
from jax import lax
from jax.experimental import pallas as pl
from jax.experimental.pallas import tpu as pltpu

F32 = jnp.float32
BF16 = jnp.bfloat16

D = 1024
DEPTH = 2
GRID_W = 64
EPS = 1e-6
HEADS = 4
HD = 128
DNW = HEADS * HD
CH = 128
N_LVL = 7
SSM_W = 512
SSM_G = 32
SSM_GC = 16
SSM_P = 64
SSM_N = SSM_G * SSM_P
SSM_BLK = 4
POOL_WINDOWS = (2, 4, 8, 16)
POOL_W = 512
POOL_G = 128
DFF = 2816
FF_BLK = 256
PAD = 8
SUB = 8
VMEM_LIMIT = 56 << 20


def _cparams(*sem):
    return pltpu.CompilerParams(dimension_semantics=sem, vmem_limit_bytes=VMEM_LIMIT)


def _const_spec(shape):
    zeros = (0,) * len(shape)
    return pl.BlockSpec(shape, lambda *_: zeros, pipeline_mode=pl.Buffered(1))


def _mm(a, b):
    return jnp.dot(a.astype(BF16), b.astype(BF16), preferred_element_type=F32)


def _mm_nt(a, b):
    return lax.dot_general(a.astype(BF16), b.astype(BF16), (((1,), (1,)), ((), ())),
                           preferred_element_type=F32)


def _mm_tn(a, b):
    return lax.dot_general(a.astype(BF16), b.astype(BF16), (((0,), (0,)), ((), ())),
                           preferred_element_type=F32)


def _mm_exact(a, b):
    return jnp.dot(a, b, precision=lax.Precision.HIGHEST, preferred_element_type=F32)


def _silu(x):
    return x * jax.nn.sigmoid(x)


def _rms(x):
    return x * lax.rsqrt(jnp.mean(x * x, axis=-1, keepdims=True) + EPS)


def _ada_kernel(c_ref, w_ref, b_ref, o_ref):
    o_ref[...] = _mm(_silu(c_ref[...]), w_ref[...]) + b_ref[...]


def _ada(conds, w_ada, b_ada):
    n = conds.shape[0]
    tn = 1536
    return pl.pallas_call(
        _ada_kernel,
        grid=(DEPTH, 6 * D // tn),
        in_specs=[pl.BlockSpec((n, D), lambda l, j: (0, 0)),
                  pl.BlockSpec((None, D, tn), lambda l, j: (l, 0, j)),
                  pl.BlockSpec((None, 1, tn), lambda l, j: (l, 0, j))],
        out_specs=pl.BlockSpec((None, n, tn), lambda l, j: (l, 0, j)),
        out_shape=jax.ShapeDtypeStruct((DEPTH, n, 6 * D), F32),
        compiler_params=_cparams("arbitrary", "arbitrary"),
        name="ada",
    )(conds, w_ada, b_ada.reshape(DEPTH, 1, 6 * D))


def _addpos_kernel(x_ref, p_ref, o_ref):
    o_ref[...] = x_ref[...] + p_ref[...]


def _add_pos(x, pos):
    b, l, _ = x.shape
    return pl.pallas_call(
        _addpos_kernel,
        grid=(b,),
        in_specs=[pl.BlockSpec((None, l, D), lambda i: (i, 0, 0)), _const_spec((l, D))],
        out_specs=pl.BlockSpec((None, l, D), lambda i: (i, 0, 0)),
        out_shape=jax.ShapeDtypeStruct(x.shape, F32),
        compiler_params=_cparams("arbitrary"),
        name="add_pos",
    )(x, pos)


IN_AB = 4 * DNW
IN_SP = IN_AB + 128
IN_GATE = IN_SP + SSM_W + POOL_W
IN_END = IN_GATE + 3 * D


def _inproj_kernel(x_ref, mod_ref, g_ref, w_ref, qkvz_ref, ab_ref, sp_ref, gate_ref):
    y = _rms(x_ref[...]) * g_ref[...]
    h = (y * (1.0 + mod_ref[0, 1:2, :]) + mod_ref[0, 0:1, :]).astype(BF16)
    qkvz_ref[...] = jnp.dot(h, w_ref[:, 0:IN_AB], preferred_element_type=F32)
    ab_ref[...] = jnp.dot(h, w_ref[:, IN_AB:IN_SP], preferred_element_type=F32)
    sp_ref[...] = jnp.dot(h, w_ref[:, IN_SP:IN_GATE], preferred_element_type=F32)
    gate_ref[...] = jnp.dot(h, w_ref[:, IN_GATE:IN_END], preferred_element_type=F32).astype(BF16)


def _pack_w_in(w_in):
    n_ab = 4 * HEADS
    return jnp.concatenate([w_in[:, :IN_AB], w_in[:, IN_AB:IN_AB + n_ab], jnp.zeros((D, 128 - n_ab), F32),
                            w_in[:, IN_AB + n_ab:]], axis=1).astype(BF16)


def _in_proj(x2, mod, seq_len, norm_g, w):
    t = x2.shape[0]
    tm = 512
    per_seq = mod.shape[0] > 1
    tiles_per_seq = seq_len // tm
    if per_seq:
        assert seq_len % tm == 0
        mod_map = lambda i: (i // tiles_per_seq, 0, 0)
    else:
        mod_map = lambda i: (0, 0, 0)
    row = lambda w: pl.BlockSpec((tm, w), lambda i: (i, 0))
    return pl.pallas_call(
        _inproj_kernel,
        grid=(t // tm,),
        in_specs=[row(D), pl.BlockSpec((1, 6, D), mod_map), _const_spec((1, D)), _const_spec(w.shape)],
        out_specs=[row(4 * DNW), row(128), row(SSM_W + POOL_W), row(3 * D)],
        out_shape=[jax.ShapeDtypeStruct((t, 4 * DNW), F32), jax.ShapeDtypeStruct((t, 128), F32),
                   jax.ShapeDtypeStruct((t, SSM_W + POOL_W), F32),
                   jax.ShapeDtypeStruct((t, 3 * D), BF16)],
        compiler_params=_cparams("arbitrary"),
        name="in_proj",
    )(x2, mod, norm_g.reshape(1, D), w)


def _dn_kernel(qkv_ref, z_ref, ab_ref, cw_ref, alog_ref, dtb_ref, ng_ref, s0_ref,
               o_ref, sfin_ref,
               pad_s, q_s, k_s, v_s, g_s, gt_s, b_s, o_s, st_s, lm_s):
    L = o_ref.shape[0]
    nc = L // CH

    pad_s[0:PAD, :] = jnp.zeros((PAD, 3 * DNW), F32)
    pad_s[PAD + L:PAD + L + PAD, :] = jnp.zeros((PAD, 3 * DNW), F32)
    pad_s[PAD:PAD + L, :] = qkv_ref[...]
    rc = 256
    for cb in range(3 * HEADS):
        cols = slice(cb * HD, (cb + 1) * HD)
        w = cw_ref[:, cols]
        for r0 in range(0, L, rc):
            y = (w[0:1] * pad_s[PAD - 1 + r0:PAD - 1 + r0 + rc, cols]
                 + w[1:2] * pad_s[PAD + r0:PAD + r0 + rc, cols]
                 + w[2:3] * pad_s[PAD + 1 + r0:PAD + 1 + r0 + rc, cols])
            y = _silu(y)
            h = cb % HEADS
            hc = slice(h * HD, (h + 1) * HD)
            if cb < HEADS:
                y = y * lax.rsqrt(jnp.sum(y * y, axis=-1, keepdims=True) + EPS) * (HD ** -0.5)
                q_s[r0:r0 + rc, hc] = y
            elif cb < 2 * HEADS:
                y = y * lax.rsqrt(jnp.sum(y * y, axis=-1, keepdims=True) + EPS)
                k_s[r0:r0 + rc, hc] = y
            else:
                v_s[r0:r0 + rc, hc] = y

    ab = ab_ref[...]
    xs = ab + dtb_ref[...]
    softplus = jnp.maximum(xs, 0.0) + jnp.log1p(jnp.exp(-jnp.abs(xs)))
    g = -jnp.exp(alog_ref[...]) * softplus
    g_s[...] = g
    gt_s[...] = g.T
    b_s[...] = jax.nn.sigmoid(ab)

    for d in range(2):
        for h in range(HEADS):
            st_s[d * HEADS + h] = s0_ref[d, h]
    o_s[...] = jnp.zeros(o_s.shape, F32)

    ri = lax.broadcasted_iota(jnp.int32, (CH, CH), 0)
    ci = lax.broadcasted_iota(jnp.int32, (CH, CH), 1)
    lo = (ri >= ci).astype(F32)
    up = (ri <= ci).astype(F32)
    eye = (ri == ci).astype(F32)
    for lvl in range(N_LVL):
        pair = ((ri >> (lvl + 1)) == (ci >> (lvl + 1))) & ((ri >> lvl) != (ci >> lvl))
        lm_s[lvl] = pair.astype(F32)

    streams = [(d, h) for d in range(2) for h in range(HEADS)]

    def chunk_step(c, carry):
        rows, gcol, grow = [], [], []
        for d in range(2):
            cidx = c if d == 0 else nc - 1 - c
            rows.append(pl.ds(pl.multiple_of(cidx * CH, CH), CH))
            tri, tri_t = (lo, up) if d == 0 else (up, lo)
            gcol.append(_mm_exact(tri, g_s[rows[d], :]))
            grow.append(_mm_exact(gt_s[:, rows[d]], tri_t))
        incl = [ri >= ci, ri <= ci]
        strict = [ri > ci, ri < ci]
        last = [CH - 1, 0]
        hcs = [slice(h * HD, (h + 1) * HD) for h in range(HEADS)]
        q, k, v, gc, gtot, beta, decay, eg, kb, a = ([] for _ in range(10))
        for d, h in streams:
            s = d * HEADS + h
            q.append(q_s[rows[d], hcs[h]])
            k.append(k_s[rows[d], hcs[h]])
            v.append(v_s[rows[d], hcs[h]])
            gc.append(gcol[d][:, s:s + 1])
            gtot.append(gcol[d][last[d]:last[d] + 1, s:s + 1])
            beta.append(b_s[rows[d], 2 * HEADS + s:2 * HEADS + s + 1])
            decay.append(jnp.where(incl[d], jnp.exp(gc[s] - grow[d][s:s + 1, :]), 0.0))
            eg.append(jnp.exp(gc[s]))
            kb.append(k[s] * beta[s])
        ns = len(streams)
        a = [jnp.where(strict[s // HEADS], _mm_nt(kb[s], k[s]) * decay[s], 0.0) for s in range(ns)]
        t = [eye - a[s] * lm_s[0] for s in range(ns)]
        for lvl in range(1, N_LVL):
            x = [_mm(a[s] * lm_s[lvl], t[s]) for s in range(ns)]
            t = [t[s] - _mm(t[s], x[s]) for s in range(ns)]
        uw = [_mm(t[s], jnp.concatenate([v[s] * beta[s], kb[s] * eg[s]], axis=1)) for s in range(ns)]
        qk = [_mm_nt(q[s], k[s]) * decay[s] for s in range(ns)]
        st = [st_s[s] for s in range(ns)]
        ws = [_mm(jnp.concatenate([uw[s][:, HD:], q[s] * eg[s]], axis=0), st[s]) for s in range(ns)]
        v_new = [uw[s][:, :HD] - ws[s][:CH] for s in range(ns)]
        o = [ws[s][CH:] + _mm(qk[s], v_new[s]) for s in range(ns)]
        for d, h in streams:
            s = d * HEADS + h
            st_s[s] = st[s] * jnp.exp(gtot[s]) + _mm_tn(k[s] * jnp.exp(gtot[s] - gc[s]), v_new[s])
        for d, h in streams:
            s = d * HEADS + h
            o_s[rows[d], hcs[h]] = o_s[rows[d], hcs[h]] + o[s]
        return carry

    lax.fori_loop(0, nc, chunk_step, 0)

    for d in range(2):
        for h in range(HEADS):
            sfin_ref[d, h] = st_s[d * HEADS + h]
    for h in range(HEADS):
        hc = slice(h * HD, (h + 1) * HD)
        for r0 in range(0, L, rc):
            z = z_ref[r0:r0 + rc, hc]
            o_ref[r0:r0 + rc, hc] = (_rms(o_s[r0:r0 + rc, hc]) * ng_ref[...] * _silu(z)).astype(BF16)


def _deltanet(qkvz, ab, n_seq, seq_len, conv_w, a_log, dt_bias, norm_g, s0):
    L = seq_len
    pad_row = lambda v: jnp.zeros((1, 128), F32).at[0, :2 * HEADS].set(v.reshape(-1))
    st_spec = pl.BlockSpec((None, 2, HEADS, HD, HD), lambda i: (i, 0, 0, 0, 0))
    return pl.pallas_call(
        _dn_kernel,
        grid=(n_seq,),
        in_specs=[pl.BlockSpec((L, 3 * DNW), lambda i: (i, 0)),
                  pl.BlockSpec((L, DNW), lambda i: (i, 3)),
                  pl.BlockSpec((L, 128), lambda i: (i, 0)),
                  _const_spec((3, 3 * DNW)), _const_spec((1, 128)), _const_spec((1, 128)),
                  _const_spec((1, HD)), st_spec],
        out_specs=[pl.BlockSpec((L, DNW), lambda i: (i, 0)), st_spec],
        out_shape=[jax.ShapeDtypeStruct((n_seq * L, DNW), BF16),
                   jax.ShapeDtypeStruct((n_seq, 2, HEADS, HD, HD), F32)],
        scratch_shapes=[pltpu.VMEM((L + 2 * PAD, 3 * DNW), F32),
                        pltpu.VMEM((L, DNW), F32), pltpu.VMEM((L, DNW), F32), pltpu.VMEM((L, DNW), F32),
                        pltpu.VMEM((L, 128), F32), pltpu.VMEM((128, L), F32), pltpu.VMEM((L, 128), F32),
                        pltpu.VMEM((L, DNW), F32), pltpu.VMEM((2 * HEADS, HD, HD), F32),
                        pltpu.VMEM((N_LVL, CH, CH), F32)],
        compiler_params=_cparams("arbitrary"),
        name="deltanet",
    )(qkvz, qkvz, ab, conv_w, pad_row(a_log), pad_row(dt_bias), norm_g.reshape(1, HD), s0)


def _s5disc_kernel(lre_ref, lim_ref, ls_ref, bre_ref, bim_ref, lam_ref, bb_ref):
    lam_re = lre_ref[...]
    lam_im = lim_ref[...]
    step = jnp.exp(ls_ref[...])
    mag = jnp.exp(lam_re * step)
    ang = lam_im * step
    lb_re = mag * jnp.cos(ang)
    lb_im = mag * jnp.sin(ang)
    nr, ni = lb_re - 1.0, lb_im
    den = lam_re * lam_re + lam_im * lam_im
    f_re = (nr * lam_re + ni * lam_im) / den
    f_im = (ni * lam_re - nr * lam_im) / den
    lam_ref[:, 0, :] = lb_re
    lam_ref[:, 1, :] = lb_im
    b_re = bre_ref[...]
    b_im = bim_ref[...]
    for d in range(2):
        fr = f_re[d:d + 1]
        fi = f_im[d:d + 1]
        bb_ref[d, 0] = fr * b_re - fi * b_im
        bb_ref[d, 1] = fr * b_im + fi * b_re


def _s5_discretise(lam_re, lam_im, log_step, b_re, b_im):
    ls = jnp.repeat(log_step, SSM_P, axis=1)
    bt = lambda b: b.reshape(SSM_N, SSM_GC).T
    return pl.pallas_call(
        _s5disc_kernel,
        out_shape=[jax.ShapeDtypeStruct((2, 2, SSM_N), F32),
                   jax.ShapeDtypeStruct((2, 2, SSM_GC, SSM_N), F32)],
        name="s5_discretise",
    )(lam_re.reshape(2, SSM_N), lam_im.reshape(2, SSM_N), ls, bt(b_re), bt(b_im))


def _s5_block_mats(bb, c_re, c_im):
    eye = jnp.eye(SUB, dtype=F32)
    b6 = bb.reshape(2, 2, SSM_GC, SSM_BLK, SUB, SSM_P)
    bd = jnp.einsum("drcjgp,gh->djgcrhp", b6, eye).reshape(2, SSM_BLK, SUB * SSM_GC, 2 * SUB * SSM_P)
    c2 = jnp.stack([c_re, -c_im], axis=0).reshape(2, SSM_BLK, SUB, SSM_GC, SSM_P)
    cm = jnp.einsum("rjgcp,gh->jrgphc", c2, eye).reshape(SSM_BLK, 2 * SUB * SSM_P, SUB * SSM_GC)
    return bd.astype(BF16), cm.astype(BF16)


def _s5_kernel(uf_ref, ub_ref, bd_ref, cm_ref, lam_ref, h0_ref, yf_ref, yb_ref, fin_ref,
               xf_s, xb_s, car_s):
    i = pl.program_id(1)
    tl = uf_ref.shape[0]
    rows = tl * SUB
    bw = 2 * SUB * SSM_P
    half = SUB * SSM_P

    @pl.when(i == 0)
    def _():
        car_s[...] = h0_ref[...]

    uf = uf_ref[...].reshape(rows, SSM_W).astype(BF16)
    ub = ub_ref[...].reshape(rows, SSM_W).astype(BF16)
    for j in range(SSM_BLK):
        cin = slice(j * SUB * SSM_GC, (j + 1) * SUB * SSM_GC)
        xf_s[:, j * bw:(j + 1) * bw] = jnp.dot(uf[:, cin], bd_ref[0, j], preferred_element_type=F32)
        xb_s[:, j * bw:(j + 1) * bw] = jnp.dot(ub[:, cin], bd_ref[1, j], preferred_element_type=F32)

    lw = 256
    for j in range(SSM_BLK):
        for kk in range(half // lw):
            cre = slice(j * bw + kk * lw, j * bw + (kk + 1) * lw)
            cim = slice(j * bw + half + kk * lw, j * bw + half + (kk + 1) * lw)
            n0 = j * half + kk * lw
            lam = [(jnp.broadcast_to(lam_ref[d, 0:1, n0:n0 + lw], (SUB, lw)),
                    jnp.broadcast_to(lam_ref[d, 1:2, n0:n0 + lw], (SUB, lw))) for d in range(2)]

            def step(t, carry, cre=cre, cim=cim, lam=lam):
                fr, fi, br, bi = carry
                rf = pl.ds(pl.multiple_of(t * SUB, SUB), SUB)
                rb = pl.ds(pl.multiple_of((tl - 1 - t) * SUB, SUB), SUB)
                (lr, li), (mr, mi) = lam
                nfr = lr * fr - li * fi + xf_s[rf, cre]
                nfi = lr * fi + li * fr + xf_s[rf, cim]
                nbr = mr * br - mi * bi + xb_s[rb, cre]
                nbi = mr * bi + mi * br + xb_s[rb, cim]
                xf_s[rf, cre] = nfr
                xf_s[rf, cim] = nfi
                xb_s[rb, cre] = nbr
                xb_s[rb, cim] = nbi
                return nfr, nfi, nbr, nbi

            init = (car_s[0, :, cre], car_s[0, :, cim], car_s[1, :, cre], car_s[1, :, cim])
            fr, fi, br, bi = lax.fori_loop(0, tl, step, init, unroll=4)
            car_s[0, :, cre] = fr
            car_s[0, :, cim] = fi
            car_s[1, :, cre] = br
            car_s[1, :, cim] = bi

    for j in range(SSM_BLK):
        cout = slice(j * SUB * SSM_GC, (j + 1) * SUB * SSM_GC)
        yf = jnp.dot(xf_s[:, j * bw:(j + 1) * bw].astype(BF16), cm_ref[j], preferred_element_type=F32)
        yb = jnp.dot(xb_s[:, j * bw:(j + 1) * bw].astype(BF16), cm_ref[j], preferred_element_type=F32)
        yf_ref[:, :, cout] = yf.reshape(tl, SUB, SUB * SSM_GC)
        yb_ref[:, :, cout] = yb.reshape(tl, SUB, SUB * SSM_GC)

    @pl.when(i == pl.num_programs(1) - 1)
    def _():
        fin_ref[...] = car_s[...]


def _s5_scan(u_tm, bd, cm, lam, h0):
    n_grp, L = u_tm.shape[0], u_tm.shape[1]
    tl = 32
    nt = L // tl
    ncol = 2 * SSM_N
    fwd = pl.BlockSpec((None, tl, SUB, SSM_W), lambda g, i: (g, i, 0, 0))
    bwd = pl.BlockSpec((None, tl, SUB, SSM_W), lambda g, i: (g, nt - 1 - i, 0, 0))
    st = pl.BlockSpec((None, 2, SUB, ncol), lambda g, i: (g, 0, 0, 0))
    return pl.pallas_call(
        _s5_kernel,
        grid=(n_grp, nt),
        in_specs=[fwd, bwd, _const_spec(bd.shape), _const_spec(cm.shape), _const_spec(lam.shape), st],
        out_specs=[fwd, bwd, st],
        out_shape=[jax.ShapeDtypeStruct(u_tm.shape, F32), jax.ShapeDtypeStruct(u_tm.shape, F32),
                   jax.ShapeDtypeStruct((n_grp, 2, SUB, ncol), F32)],
        scratch_shapes=[pltpu.VMEM((tl * SUB, ncol), F32), pltpu.VMEM((tl * SUB, ncol), F32),
                        pltpu.VMEM((2, SUB, ncol), F32)],
        compiler_params=_cparams("arbitrary", "arbitrary"),
        name="s5_scan",
    )(u_tm, u_tm, bd, cm, lam, h0)


def _state_to_cols(h_re, h_im):
    b = h_re.shape[0]
    r = h_re.reshape(b, 2, SSM_BLK, SUB * SSM_P)
    m = h_im.reshape(b, 2, SSM_BLK, SUB * SSM_P)
    return jnp.stack([r, m], axis=3).reshape(b, 2, 2 * SSM_N)


def _cols_to_state(cols):
    b = cols.shape[0]
    c = cols.reshape(b, 2, SSM_BLK, 2, SUB * SSM_P)
    return (c[:, :, :, 0].reshape(b, 2, SSM_G, SSM_P), c[:, :, :, 1].reshape(b, 2, SSM_G, SSM_P))


def _pool_kernel(sp_ref, o_ref, pad_s):
    L = o_ref.shape[0]
    pad_s[0:PAD, :] = jnp.zeros((PAD, POOL_W), F32)
    pad_s[PAD + L:PAD + L + PAD, :] = jnp.zeros((PAD, POOL_W), F32)
    pad_s[PAD:PAD + L, :] = sp_ref[...]
    rc = 256
    for gi, w in enumerate(POOL_WINDOWS):
        cols = slice(gi * POOL_G, (gi + 1) * POOL_G)
        for r0 in range(0, L, rc):
            acc = pad_s[PAD + r0 - w // 2:PAD + r0 - w // 2 + rc, cols]
            for dlt in range(-(w // 2) + 1, w // 2):
                acc = acc + pad_s[PAD + r0 + dlt:PAD + r0 + dlt + rc, cols]
            t = r0 + lax.broadcasted_iota(jnp.int32, (rc, POOL_G), 0)
            cnt = jnp.minimum(t + w // 2, L) - jnp.maximum(t - w // 2, 0)
            pooled = acc / cnt.astype(F32) - pad_s[PAD + r0:PAD + r0 + rc, cols]
            o_ref[r0:r0 + rc, cols] = pooled.astype(BF16)


def _pool(sp, n_seq, seq_len):
    L = seq_len
    return pl.pallas_call(
        _pool_kernel,
        grid=(n_seq,),
        in_specs=[pl.BlockSpec((L, POOL_W), lambda i: (i, 1))],
        out_specs=pl.BlockSpec((L, POOL_W), lambda i: (i, 0)),
        out_shape=jax.ShapeDtypeStruct((n_seq * L, POOL_W), BF16),
        scratch_shapes=[pltpu.VMEM((L + 2 * PAD, POOL_W), F32)],
        compiler_params=_cparams("arbitrary"),
        name="pool",
    )(sp)


def _merge_kernel(x_ref, odn_ref, yf_ref, yb_ref, sp_ref, pooled_ref, gate_ref, mod_ref,
                  sd_ref, gw_ref, gb_ref, pw_ref, ps_ref, wdn_ref, wssm_ref, wpool_ref, wout_ref, o_ref):
    y = yf_ref[...] + yb_ref[...] + sd_ref[...] * sp_ref[...]
    y = jax.nn.gelu(y)
    o_ssm = y * jax.nn.sigmoid(_mm(y, gw_ref[...]) + gb_ref[...])
    o_pool = jnp.concatenate(
        [_mm(pooled_ref[:, gi * POOL_G:(gi + 1) * POOL_G], pw_ref[gi]) for gi in range(len(POOL_WINDOWS))],
        axis=1) * ps_ref[...]
    gate = lambda i: jax.nn.sigmoid(gate_ref[:, i * D:(i + 1) * D].astype(F32))
    merged = (gate(0) * _mm(odn_ref[...], wdn_ref[...]) + gate(1) * _mm(o_ssm, wssm_ref[...])
              + gate(2) * _mm(o_pool, wpool_ref[...]))
    o_ref[...] = x_ref[...] + mod_ref[0, 2:3, :] * _mm(merged, wout_ref[...])


def _merge(x2, odn, yf, yb, sp, pooled, gates, mod, seq_len, ssm_d, glu_w, glu_b, pool_w, pool_scale,
           wdn, wssm, wpool, wout):
    t = x2.shape[0]
    tm = 512
    per_seq = mod.shape[0] > 1
    tiles_per_seq = seq_len // tm
    mod_map = (lambda i: (i // tiles_per_seq, 0, 0)) if per_seq else (lambda i: (0, 0, 0))
    row = lambda w: pl.BlockSpec((tm, w), lambda i: (i, 0))
    return pl.pallas_call(
        _merge_kernel,
        grid=(t // tm,),
        in_specs=[row(D), row(DNW), row(SSM_W), row(SSM_W), row(SSM_W), row(POOL_W), row(3 * D),
                  pl.BlockSpec((1, 6, D), mod_map),
                  _const_spec((1, SSM_W)), _const_spec(glu_w.shape), _const_spec((1, SSM_W)),
                  _const_spec(pool_w.shape), _const_spec((1, POOL_W)),
                  _const_spec(wdn.shape), _const_spec(wssm.shape), _const_spec(wpool.shape),
                  _const_spec(wout.shape)],
        out_specs=row(D),
        out_shape=jax.ShapeDtypeStruct((t, D), F32),
        compiler_params=_cparams("arbitrary"),
        name="merge",
    )(x2, odn, yf, yb, sp, pooled, gates, mod, ssm_d.reshape(1, SSM_W), glu_w, glu_b.reshape(1, SSM_W),
      pool_w, pool_scale.reshape(1, POOL_W), wdn, wssm, wpool, wout)


def _ffn_kernel(n_seq, final, x_ref, mod_ref, g_ref, wup_ref, cw_ref, wdn_ref, fg_ref, o_ref,
                h_s, pad0_s, pad1_s, act0_s, act1_s, acc_s):
    rows = x_ref.shape[0]
    L = rows // n_seq
    tm = 256
    pad_s = (pad0_s, pad1_s)
    act_s = (act0_s, act1_s)

    def norm_tile(it, carry):
        rs = pl.ds(pl.multiple_of(it * tm, tm), tm)
        y = _rms(x_ref[rs, :]) * g_ref[...]
        h_s[rs, :] = (y * (1.0 + mod_ref[0, 4:5, :]) + mod_ref[0, 3:4, :]).astype(BF16)
        return carry

    lax.fori_loop(0, rows // tm, norm_tile, 0)

    for slot in range(2):
        for n in range(n_seq):
            pad_s[slot][n, 0:PAD, :] = jnp.zeros((PAD, 2 * FF_BLK), F32)
            pad_s[slot][n, PAD + L:PAD + L + PAD, :] = jnp.zeros((PAD, 2 * FF_BLK), F32)
    acc_s[...] = jnp.zeros(acc_s.shape, F32)
    rc = min(L, 256)

    def gate_cols(j):
        return pl.ds(pl.multiple_of(j * FF_BLK, FF_BLK), FF_BLK)

    def val_cols(j):
        return pl.ds(pl.multiple_of(DFF + j * FF_BLK, 128), FF_BLK)

    def up(j, slot):
        h = h_s[...]
        up_g = jnp.dot(h, wup_ref[:, gate_cols(j)], preferred_element_type=F32)
        up_v = jnp.dot(h, wup_ref[:, val_cols(j)], preferred_element_type=F32)
        for n in range(n_seq):
            pad_s[slot][n, PAD:PAD + L, 0:FF_BLK] = up_g[n * L:(n + 1) * L]
            pad_s[slot][n, PAD:PAD + L, FF_BLK:2 * FF_BLK] = up_v[n * L:(n + 1) * L]

    def conv_act(j, slot):
        wg = cw_ref[:, gate_cols(j)]
        wv = cw_ref[:, val_cols(j)]
        for n in range(n_seq):
            for r0 in range(0, L, rc):
                def conv(w, cols):
                    return (w[0:1] * pad_s[slot][n, PAD - 1 + r0:PAD - 1 + r0 + rc, cols]
                            + w[1:2] * pad_s[slot][n, PAD + r0:PAD + r0 + rc, cols]
                            + w[2:3] * pad_s[slot][n, PAD + 1 + r0:PAD + 1 + r0 + rc, cols])
                hg = conv(wg, slice(0, FF_BLK))
                hv = conv(wv, slice(FF_BLK, 2 * FF_BLK))
                act_s[slot][n * L + r0:n * L + r0 + rc, :] = (_silu(hg) * hv).astype(BF16)

    def down(j, slot):
        acc_s[...] += jnp.dot(act_s[slot][...], wdn_ref[gate_cols(j), :], preferred_element_type=F32)

    n_blk = DFF // FF_BLK
    assert n_blk % 2 == 1
    up(0, 0)

    def col_pair(jj, carry):
        j = 2 * jj
        up(j + 1, 1)
        conv_act(j, 0)
        down(j, 0)
        up(j + 2, 0)
        conv_act(j + 1, 1)
        down(j + 1, 1)
        return carry

    lax.fori_loop(0, n_blk // 2, col_pair, 0)
    conv_act(n_blk - 1, 0)
    down(n_blk - 1, 0)

    def out_tile(it, carry):
        rs = pl.ds(pl.multiple_of(it * tm, tm), tm)
        y = x_ref[rs, :] + mod_ref[0, 5:6, :] * acc_s[rs, :]
        if final:
            y = _rms(y) * fg_ref[...]
        o_ref[rs, :] = y
        return carry

    lax.fori_loop(0, rows // tm, out_tile, 0)


def _ffn(x2, mod, n_tile_seq, seq_len, norm_g, wup, conv_w, wdn, final_g, final):
    t = x2.shape[0]
    rows = n_tile_seq * seq_len
    per_seq = mod.shape[0] > 1
    mod_map = (lambda i: (i, 0, 0)) if per_seq else (lambda i: (0, 0, 0))
    if per_seq:
        assert n_tile_seq == 1
    row = pl.BlockSpec((rows, D), lambda i: (i, 0))
    return pl.pallas_call(
        functools.partial(_ffn_kernel, n_tile_seq, final),
        grid=(t // rows,),
        in_specs=[row, pl.BlockSpec((1, 6, D), mod_map), _const_spec((1, D)), _const_spec(wup.shape),
                  _const_spec(conv_w.shape), _const_spec(wdn.shape), _const_spec((1, D))],
        out_specs=row,
        out_shape=jax.ShapeDtypeStruct((t, D), F32),
        scratch_shapes=[pltpu.VMEM((rows, D), BF16),
                        pltpu.VMEM((n_tile_seq, seq_len + 2 * PAD, 2 * FF_BLK), F32),
                        pltpu.VMEM((n_tile_seq, seq_len + 2 * PAD, 2 * FF_BLK), F32),
                        pltpu.VMEM((rows, FF_BLK), BF16), pltpu.VMEM((rows, FF_BLK), BF16),
                        pltpu.VMEM((rows, D), F32)],
        compiler_params=_cparams("arbitrary"),
        name="conv_ffn",
    )(x2, mod, norm_g.reshape(1, D), wup, conv_w, wdn, final_g.reshape(1, D))


def _grid_pos_embed(rows, dim):
    quarter = dim // 4
    omega = 1.0 / (10000.0 ** (jnp.arange(quarter, dtype=F32) / quarter))
    r = jnp.broadcast_to(jnp.arange(rows, dtype=F32)[:, None], (rows, GRID_W)).reshape(-1)
    col = jnp.broadcast_to(jnp.arange(GRID_W, dtype=F32)[None, :], (rows, GRID_W)).reshape(-1)

    def sincos(p):
        ang = p[:, None] * omega[None, :]
        return jnp.concatenate([jnp.sin(ang), jnp.cos(ang)], axis=-1)

    return jnp.concatenate([sincos(r), sincos(col)], axis=-1)


def _to_time_major(u2, n_seq, seq_len):
    w = u2.shape[-1]
    return u2.reshape(n_seq // SUB, SUB, seq_len, w).transpose(0, 2, 1, 3)


def _from_time_major(y, n_seq, seq_len):
    w = y.shape[-1]
    return y.transpose(0, 2, 1, 3).reshape(n_seq * seq_len, w)


def _layer(x2, mod, n_seq, seq_len, n_tile_seq, s_dn, h_cols, p, final):
    qkvz, ab, sp, gates = _in_proj(x2, mod, seq_len, p["norm1_g"], p["w_in"])
    odn, st_dn = _deltanet(qkvz, ab, n_seq, seq_len, p["dn_conv"], p["dn_a_log"], p["dn_dt_bias"],
                           p["dn_norm_g"], s_dn)
    u_tm = _to_time_major(sp[:, :SSM_W], n_seq, seq_len)
    h0 = h_cols.reshape(n_seq // SUB, SUB, 2, 2 * SSM_N).transpose(0, 2, 1, 3)
    yf, yb, fin = _s5_scan(u_tm, p["bd"], p["cm"], p["lam"], h0)
    yf = _from_time_major(yf, n_seq, seq_len)
    yb = _from_time_major(yb, n_seq, seq_len)
    fin = fin.transpose(0, 2, 1, 3).reshape(n_seq, 2, 2 * SSM_N)
    pooled = _pool(sp, n_seq, seq_len)
    x2 = _merge(x2, odn, yf, yb, sp, pooled, gates, mod, seq_len, p["ssm_d"], p["glu_w"], p["ssm_glu_b"],
                p["pool_w"], p["pool_scale"], p["wdn"], p["wssm"], p["wpool"], p["wout"])
    x2 = _ffn(x2, mod, n_tile_seq, seq_len, p["norm2_g"], p["wup"], p["ffn_conv"], p["wdown"],
              p["final_norm_g"], final)
    return x2, st_dn, fin


def kernel(x_prompt, x_sample, state_dn, state_ssm_re, state_ssm_im, c, c_ctx, norm1_g, norm2_g, w_ada, b_ada, w_in, dn_conv, dn_a_log, dn_dt_bias, dn_norm_g, ssm_lambda_re, ssm_lambda_im, ssm_log_step, ssm_b_re, ssm_b_im, ssm_c_re, ssm_c_im, ssm_d, ssm_glu_w, ssm_glu_b, pool_w, pool_scale, w_branch_dn, w_branch_ssm, w_branch_pool, w_out, ffn_w_up, ffn_conv, ffn_w_down, final_norm_g):
    bc, lc, _ = x_prompt.shape
    bl, ll, _ = x_sample.shape

    n_cond = 16
    conds = jnp.zeros((n_cond, D), F32).at[0].set(c_ctx).at[1:1 + bl].set(c)
    mods = _ada(conds, w_ada, b_ada).reshape(DEPTH, n_cond, 6, D)

    x_ctx = x_prompt.reshape(bc * lc, D)
    x_lat = _add_pos(x_sample, _grid_pos_embed(ll // GRID_W, D)).reshape(bl * ll, D)

    zero_dn = jnp.zeros((bc, 2, HEADS, HD, HD), F32)
    zero_h = jnp.zeros((bc, 2, 2 * SSM_N), F32)
    new_dn, new_re, new_im = [], [], []
    for l in range(DEPTH):
        lam, bb = _s5_discretise(ssm_lambda_re[l], ssm_lambda_im[l], ssm_log_step[l], ssm_b_re[l], ssm_b_im[l])
        bd, cm = _s5_block_mats(bb, ssm_c_re[l], ssm_c_im[l])
        p = dict(
            norm1_g=norm1_g[l], norm2_g=norm2_g[l], final_norm_g=final_norm_g, w_in=_pack_w_in(w_in[l]),
            dn_conv=dn_conv[l], dn_a_log=dn_a_log[l], dn_dt_bias=dn_dt_bias[l], dn_norm_g=dn_norm_g[l],
            bd=bd, cm=cm, lam=lam, ssm_d=ssm_d[l], glu_w=ssm_glu_w[l].astype(BF16), ssm_glu_b=ssm_glu_b[l],
            pool_w=pool_w[l].astype(BF16), pool_scale=pool_scale[l],
            wdn=w_branch_dn[l].astype(BF16), wssm=w_branch_ssm[l].astype(BF16),
            wpool=w_branch_pool[l].astype(BF16), wout=w_out[l].astype(BF16),
            wup=ffn_w_up[l].astype(BF16), ffn_conv=ffn_conv[l], wdown=ffn_w_down[l].astype(BF16),
        )
        final = l == DEPTH - 1
        x_ctx, st_dn, fin = _layer(x_ctx, mods[l, 0:1], bc, lc, 4, zero_dn, zero_h, p, final)
        new_dn.append(st_dn)
        f_re, f_im = _cols_to_state(fin)
        new_re.append(f_re)
        new_im.append(f_im)
        h_lat = _state_to_cols(state_ssm_re[:, l], state_ssm_im[:, l])
        x_lat, _, _ = _layer(x_lat, mods[l, 1:1 + bl], bl, ll, 1, state_dn[:, l], h_lat, p, final)

    return (x_ctx.reshape(bc, lc, D), x_lat.reshape(bl, ll, D), jnp.stack(new_dn, axis=1),
            jnp.stack(new_re, axis=1), jnp.stack(new_im, axis=1))
```

```python
import functools

import jax
import jax.numpy as jnp
from jax import lax
from jax.experimental import pallas as pl
from jax.experimental.pallas import tpu as pltpu

F32 = jnp.float32
BF16 = jnp.bfloat16

D = 1024
DEPTH = 2
GRID_W = 64
EPS = 1e-6
HEADS = 4
HD = 128
DNW = HEADS * HD
CH = 128
N_LVL = 7
SSM_W = 512
SSM_G = 32
SSM_GC = 16
SSM_P = 64
SSM_N = SSM_G * SSM_P
SSM_BLK = 4
POOL_WINDOWS = (2, 4, 8, 16)
POOL_W = 512
POOL_G = 128
DFF = 2816
FF_BLK = 256
PAD = 8
SUB = 8
VMEM_LIMIT = 56 << 20


def _cparams(*sem):
    return pltpu.CompilerParams(dimension_semantics=sem, vmem_limit_bytes=VMEM_LIMIT)


def _const_spec(shape):
    zeros = (0,) * len(shape)
    return pl.BlockSpec(shape, lambda *_: zeros, pipeline_mode=pl.Buffered(1))


def _mm(a, b):
    return jnp.dot(a.astype(BF16), b.astype(BF16), preferred_element_type=F32)


def _mm_nt(a, b):
    return lax.dot_general(a.astype(BF16), b.astype(BF16), (((1,), (1,)), ((), ())),
                           preferred_element_type=F32)


def _mm_tn(a, b):
    return lax.dot_general(a.astype(BF16), b.astype(BF16), (((0,), (0,)), ((), ())),
                           preferred_element_type=F32)


def _mm_exact(a, b):
    return jnp.dot(a, b, precision=lax.Precision.HIGHEST, preferred_element_type=F32)


def _silu(x):
    return x * jax.nn.sigmoid(x)


def _rms(x):
    return x * lax.rsqrt(jnp.mean(x * x, axis=-1, keepdims=True) + EPS)


def _ada_kernel(c_ref, w_ref, b_ref, o_ref):
    o_ref[...] = _mm(_silu(c_ref[...]), w_ref[...]) + b_ref[...]


def _ada(conds, w_ada, b_ada):
    n = conds.shape[0]
    tn = 1536
    return pl.pallas_call(
        _ada_kernel,
        grid=(DEPTH, 6 * D // tn),
        in_specs=[pl.BlockSpec((n, D), lambda l, j: (0, 0)),
                  pl.BlockSpec((None, D, tn), lambda l, j: (l, 0, j)),
                  pl.BlockSpec((None, 1, tn), lambda l, j: (l, 0, j))],
        out_specs=pl.BlockSpec((None, n, tn), lambda l, j: (l, 0, j)),
        out_shape=jax.ShapeDtypeStruct((DEPTH, n, 6 * D), F32),
        compiler_params=_cparams("arbitrary", "arbitrary"),
        name="ada",
    )(conds, w_ada, b_ada.reshape(DEPTH, 1, 6 * D))


def _addpos_kernel(x_ref, p_ref, o_ref):
    o_ref[...] = x_ref[...] + p_ref[...]


def _add_pos(x, pos):
    b, l, _ = x.shape
    return pl.pallas_call(
        _addpos_kernel,
        grid=(b,),
        in_specs=[pl.BlockSpec((None, l, D), lambda i: (i, 0, 0)), _const_spec((l, D))],
        out_specs=pl.BlockSpec((None, l, D), lambda i: (i, 0, 0)),
        out_shape=jax.ShapeDtypeStruct(x.shape, F32),
        compiler_params=_cparams("arbitrary"),
        name="add_pos",
    )(x, pos)


IN_AB = 4 * DNW
IN_SSM = IN_AB + 128
IN_POOL = IN_SSM + SSM_W
IN_GATE = IN_POOL + POOL_W
IN_END = IN_GATE + 3 * D


def _inproj_kernel(x_ref, mod_ref, g_ref, w_ref, qkvz_ref, ab_ref, ussm_ref, upool_ref, gate_ref):
    y = _rms(x_ref[...]) * g_ref[...]
    h = (y * (1.0 + mod_ref[0, 1:2, :]) + mod_ref[0, 0:1, :]).astype(BF16)
    qkvz_ref[...] = jnp.dot(h, w_ref[:, 0:IN_AB], preferred_element_type=F32)
    ab_ref[...] = jnp.dot(h, w_ref[:, IN_AB:IN_SSM], preferred_element_type=F32)
    ussm_ref[...] = jnp.dot(h, w_ref[:, IN_SSM:IN_POOL], preferred_element_type=F32)
    upool_ref[...] = jnp.dot(h, w_ref[:, IN_POOL:IN_GATE], preferred_element_type=F32)
    gate_ref[...] = jnp.dot(h, w_ref[:, IN_GATE:IN_END], preferred_element_type=F32).astype(BF16)


TOKEN_TILE = 512


def _pack_w_in(w_in):
    n_ab = 4 * HEADS
    return jnp.concatenate([w_in[:, :IN_AB], w_in[:, IN_AB:IN_AB + n_ab], jnp.zeros((D, 128 - n_ab), F32),
                            w_in[:, IN_AB + n_ab:]], axis=1).astype(BF16)


def _mod_map(mod, seq_len, tm):
    if mod.shape[0] == 1:
        return lambda i: (0, 0, 0)
    assert seq_len % tm == 0
    return lambda i: (i // (seq_len // tm), 0, 0)


def _in_proj(x2, mod, seq_len, norm_g, w):
    t = x2.shape[0]
    tm = TOKEN_TILE
    mod_map = _mod_map(mod, seq_len, tm)
    row = lambda w: pl.BlockSpec((tm, w), lambda i: (i, 0))
    return pl.pallas_call(
        _inproj_kernel,
        grid=(t // tm,),
        in_specs=[row(D), pl.BlockSpec((1, 6, D), mod_map), _const_spec((1, D)), _const_spec(w.shape)],
        out_specs=[row(4 * DNW), row(128), row(SSM_W), row(POOL_W), row(3 * D)],
        out_shape=[jax.ShapeDtypeStruct((t, 4 * DNW), F32), jax.ShapeDtypeStruct((t, 128), F32),
                   jax.ShapeDtypeStruct((t, SSM_W), F32),
                   jax.ShapeDtypeStruct((t, POOL_W), F32),
                   jax.ShapeDtypeStruct((t, 3 * D), BF16)],
        compiler_params=_cparams("arbitrary"),
        name="in_proj",
    )(x2, mod, norm_g.reshape(1, D), w)


def _dn_kernel(qkv_ref, z_ref, ab_ref, cw_ref, alog_ref, dtb_ref, ng_ref, s0_ref,
               o_ref, sfin_ref,
               pad_s, q_s, k_s, v_s, g_s, gt_s, b_s, o_s, st_s, lm_s):
    L = o_ref.shape[0]
    nc = L // CH

    pad_s[0:PAD, :] = jnp.zeros((PAD, 3 * DNW), F32)
    pad_s[PAD + L:PAD + L + PAD, :] = jnp.zeros((PAD, 3 * DNW), F32)
    pad_s[PAD:PAD + L, :] = qkv_ref[...]
    rc = 256
    for cb in range(3 * HEADS):
        cols = slice(cb * HD, (cb + 1) * HD)
        w = cw_ref[:, cols]
        for r0 in range(0, L, rc):
            y = (w[0:1] * pad_s[PAD - 1 + r0:PAD - 1 + r0 + rc, cols]
                 + w[1:2] * pad_s[PAD + r0:PAD + r0 + rc, cols]
                 + w[2:3] * pad_s[PAD + 1 + r0:PAD + 1 + r0 + rc, cols])
            y = _silu(y)
            h = cb % HEADS
            hc = slice(h * HD, (h + 1) * HD)
            if cb < HEADS:
                y = y * lax.rsqrt(jnp.sum(y * y, axis=-1, keepdims=True) + EPS) * (HD ** -0.5)
                q_s[r0:r0 + rc, hc] = y
            elif cb < 2 * HEADS:
                y = y * lax.rsqrt(jnp.sum(y * y, axis=-1, keepdims=True) + EPS)
                k_s[r0:r0 + rc, hc] = y
            else:
                v_s[r0:r0 + rc, hc] = y

    ab = ab_ref[...]
    xs = ab + dtb_ref[...]
    softplus = jnp.maximum(xs, 0.0) + jnp.log1p(jnp.exp(-jnp.abs(xs)))
    g = -jnp.exp(alog_ref[...]) * softplus
    g_s[...] = g
    gt_s[...] = g.T
    b_s[...] = jax.nn.sigmoid(ab)

    for d in range(2):
        for h in range(HEADS):
            st_s[d * HEADS + h] = s0_ref[d, h]
    o_s[...] = jnp.zeros(o_s.shape, F32)

    ri = lax.broadcasted_iota(jnp.int32, (CH, CH), 0)
    ci = lax.broadcasted_iota(jnp.int32, (CH, CH), 1)
    lo = (ri >= ci).astype(F32)
    up = (ri <= ci).astype(F32)
    eye = (ri == ci).astype(F32)
    for lvl in range(N_LVL):
        pair = ((ri >> (lvl + 1)) == (ci >> (lvl + 1))) & ((ri >> lvl) != (ci >> lvl))
        lm_s[lvl] = pair.astype(F32)

    streams = [(d, h) for d in range(2) for h in range(HEADS)]

    def chunk_step(c, carry):
        rows, gcol, grow = [], [], []
        for d in range(2):
            cidx = c if d == 0 else nc - 1 - c
            rows.append(pl.ds(pl.multiple_of(cidx * CH, CH), CH))
            tri, tri_t = (lo, up) if d == 0 else (up, lo)
            gcol.append(_mm_exact(tri, g_s[rows[d], :]))
            grow.append(_mm_exact(gt_s[:, rows[d]], tri_t))
        incl = [ri >= ci, ri <= ci]
        strict = [ri > ci, ri < ci]
        last = [CH - 1, 0]
        hcs = [slice(h * HD, (h + 1) * HD) for h in range(HEADS)]
        q, k, v, gc, gtot, beta, decay, eg, kb, a = ([] for _ in range(10))
        for d, h in streams:
            s = d * HEADS + h
            q.append(q_s[rows[d], hcs[h]])
            k.append(k_s[rows[d], hcs[h]])
            v.append(v_s[rows[d], hcs[h]])
            gc.append(gcol[d][:, s:s + 1])
            gtot.append(gcol[d][last[d]:last[d] + 1, s:s + 1])
            beta.append(b_s[rows[d], 2 * HEADS + s:2 * HEADS + s + 1])
            decay.append(jnp.where(incl[d], jnp.exp(gc[s] - grow[d][s:s + 1, :]), 0.0))
            eg.append(jnp.exp(gc[s]))
            kb.append(k[s] * beta[s])
        ns = len(streams)
        a = [jnp.where(strict[s // HEADS], _mm_nt(kb[s], k[s]) * decay[s], 0.0) for s in range(ns)]
        t = [eye - a[s] * lm_s[0] for s in range(ns)]
        for lvl in range(1, N_LVL):
            x = [_mm(a[s] * lm_s[lvl], t[s]) for s in range(ns)]
            t = [t[s] - _mm(t[s], x[s]) for s in range(ns)]
        uw = [_mm(t[s], jnp.concatenate([v[s] * beta[s], kb[s] * eg[s]], axis=1)) for s in range(ns)]
        qk = [_mm_nt(q[s], k[s]) * decay[s] for s in range(ns)]
        st = [st_s[s] for s in range(ns)]
        ws = [_mm(jnp.concatenate([uw[s][:, HD:], q[s] * eg[s]], axis=0), st[s]) for s in range(ns)]
        v_new = [uw[s][:, :HD] - ws[s][:CH] for s in range(ns)]
        o = [ws[s][CH:] + _mm(qk[s], v_new[s]) for s in range(ns)]
        for d, h in streams:
            s = d * HEADS + h
            st_s[s] = st[s] * jnp.exp(gtot[s]) + _mm_tn(k[s] * jnp.exp(gtot[s] - gc[s]), v_new[s])
        for d, h in streams:
            s = d * HEADS + h
            o_s[rows[d], hcs[h]] = o_s[rows[d], hcs[h]] + o[s]
        return carry

    lax.fori_loop(0, nc, chunk_step, 0)

    for d in range(2):
        for h in range(HEADS):
            sfin_ref[d, h] = st_s[d * HEADS + h]
    for h in range(HEADS):
        hc = slice(h * HD, (h + 1) * HD)
        for r0 in range(0, L, rc):
            z = z_ref[r0:r0 + rc, hc]
            o_ref[r0:r0 + rc, hc] = (_rms(o_s[r0:r0 + rc, hc]) * ng_ref[...] * _silu(z)).astype(BF16)


def _deltanet(qkvz, ab, n_seq, seq_len, conv_w, a_log, dt_bias, norm_g, s0):
    L = seq_len
    pad_row = lambda v: jnp.zeros((1, 128), F32).at[0, :2 * HEADS].set(v.reshape(-1))
    st_spec = pl.BlockSpec((None, 2, HEADS, HD, HD), lambda i: (i, 0, 0, 0, 0))
    return pl.pallas_call(
        _dn_kernel,
        grid=(n_seq,),
        in_specs=[pl.BlockSpec((L, 3 * DNW), lambda i: (i, 0)),
                  pl.BlockSpec((L, DNW), lambda i: (i, 3)),
                  pl.BlockSpec((L, 128), lambda i: (i, 0)),
                  _const_spec((3, 3 * DNW)), _const_spec((1, 128)), _const_spec((1, 128)),
                  _const_spec((1, HD)), st_spec],
        out_specs=[pl.BlockSpec((L, DNW), lambda i: (i, 0)), st_spec],
        out_shape=[jax.ShapeDtypeStruct((n_seq * L, DNW), BF16),
                   jax.ShapeDtypeStruct((n_seq, 2, HEADS, HD, HD), F32)],
        scratch_shapes=[pltpu.VMEM((L + 2 * PAD, 3 * DNW), F32),
                        pltpu.VMEM((L, DNW), F32), pltpu.VMEM((L, DNW), F32), pltpu.VMEM((L, DNW), F32),
                        pltpu.VMEM((L, 128), F32), pltpu.VMEM((128, L), F32), pltpu.VMEM((L, 128), F32),
                        pltpu.VMEM((L, DNW), F32), pltpu.VMEM((2 * HEADS, HD, HD), F32),
                        pltpu.VMEM((N_LVL, CH, CH), F32)],
        compiler_params=_cparams("arbitrary"),
        name="deltanet",
    )(qkvz, qkvz, ab, conv_w, pad_row(a_log), pad_row(dt_bias), norm_g.reshape(1, HD), s0)


def _s5disc_kernel(lre_ref, lim_ref, ls_ref, bre_ref, bim_ref, lam_ref, bb_ref):
    lam_re = lre_ref[...]
    lam_im = lim_ref[...]
    step = jnp.exp(ls_ref[...])
    mag = jnp.exp(lam_re * step)
    ang = lam_im * step
    lb_re = mag * jnp.cos(ang)
    lb_im = mag * jnp.sin(ang)
    nr, ni = lb_re - 1.0, lb_im
    den = lam_re * lam_re + lam_im * lam_im
    f_re = (nr * lam_re + ni * lam_im) / den
    f_im = (ni * lam_re - nr * lam_im) / den
    lam_ref[:, 0, :] = lb_re
    lam_ref[:, 1, :] = lb_im
    b_re = bre_ref[...]
    b_im = bim_ref[...]
    for d in range(2):
        fr = f_re[d:d + 1]
        fi = f_im[d:d + 1]
        bb_ref[d, 0] = fr * b_re - fi * b_im
        bb_ref[d, 1] = fr * b_im + fi * b_re


def _s5_discretise(lam_re, lam_im, log_step, b_re, b_im):
    ls = jnp.repeat(log_step, SSM_P, axis=1)
    bt = lambda b: b.reshape(SSM_N, SSM_GC).T
    return pl.pallas_call(
        _s5disc_kernel,
        out_shape=[jax.ShapeDtypeStruct((2, 2, SSM_N), F32),
                   jax.ShapeDtypeStruct((2, 2, SSM_GC, SSM_N), F32)],
        name="s5_discretise",
    )(lam_re.reshape(2, SSM_N), lam_im.reshape(2, SSM_N), ls, bt(b_re), bt(b_im))


def _s5_block_mats(bb, c_re, c_im):
    eye = jnp.eye(SUB, dtype=F32)
    b6 = bb.reshape(2, 2, SSM_GC, SSM_BLK, SUB, SSM_P)
    bd = jnp.einsum("drcjgp,gh->djgcrhp", b6, eye).reshape(2, SSM_BLK, SUB * SSM_GC, 2 * SUB * SSM_P)
    c2 = jnp.stack([c_re, -c_im], axis=0).reshape(2, SSM_BLK, SUB, SSM_GC, SSM_P)
    cm = jnp.einsum("rjgcp,gh->jrgphc", c2, eye).reshape(SSM_BLK, 2 * SUB * SSM_P, SUB * SSM_GC)
    return bd.astype(BF16), cm.astype(BF16)


def _s5_kernel(uf_ref, ub_ref, bd_ref, cm_ref, lam_ref, h0_ref, yf_ref, yb_ref, fin_ref,
               xf_s, xb_s, uf_s, ub_s, yf_s, yb_s, car_s):
    i = pl.program_id(1)
    tl = uf_ref.shape[1]
    rows = tl * SUB
    bw = 2 * SUB * SSM_P
    half = SUB * SSM_P

    @pl.when(i == 0)
    def _():
        car_s[...] = h0_ref[...]

    def seq_rows(s):
        return pl.ds(s, tl, stride=SUB)

    blk_w = SUB * SSM_GC

    def expand(d, u_ref, ut_s, x_s):
        for s in range(SUB):
            for j in range(SSM_BLK):
                ut_s[j, seq_rows(s), :] = u_ref[s, :, j * blk_w:(j + 1) * blk_w]
        for j in range(SSM_BLK):
            x_s[:, j * bw:(j + 1) * bw] = jnp.dot(ut_s[j].astype(BF16), bd_ref[d, j],
                                                  preferred_element_type=F32)

    def scan(d, x_s):
        lw = 256
        for j in range(SSM_BLK):
            for kk in range(half // lw):
                cre = slice(j * bw + kk * lw, j * bw + (kk + 1) * lw)
                cim = slice(j * bw + half + kk * lw, j * bw + half + (kk + 1) * lw)
                n0 = j * half + kk * lw
                lr = jnp.broadcast_to(lam_ref[d, 0:1, n0:n0 + lw], (SUB, lw))
                li = jnp.broadcast_to(lam_ref[d, 1:2, n0:n0 + lw], (SUB, lw))
                xr = car_s[d, :, cre]
                xi = car_s[d, :, cim]
                for t in (range(tl) if d == 0 else range(tl - 1, -1, -1)):
                    r = slice(t * SUB, (t + 1) * SUB)
                    xr, xi = lr * xr - li * xi + x_s[r, cre], lr * xi + li * xr + x_s[r, cim]
                    x_s[r, cre] = xr
                    x_s[r, cim] = xi
                car_s[d, :, cre] = xr
                car_s[d, :, cim] = xi

    def project(x_s, yt_s, y_ref):
        for j in range(SSM_BLK):
            yt_s[j] = jnp.dot(x_s[:, j * bw:(j + 1) * bw].astype(BF16), cm_ref[j], preferred_element_type=F32)
        for s in range(SUB):
            for j in range(SSM_BLK):
                y_ref[s, :, j * blk_w:(j + 1) * blk_w] = yt_s[j, seq_rows(s), :]

    expand(0, uf_ref, uf_s, xf_s)
    scan(0, xf_s)
    expand(1, ub_ref, ub_s, xb_s)
    project(xf_s, yf_s, yf_ref)
    scan(1, xb_s)
    project(xb_s, yb_s, yb_ref)

    @pl.when(i == pl.num_programs(1) - 1)
    def _():
        fin_ref[...] = car_s[...]


def _s5_scan(u, bd, cm, lam, h0):
    n_seq, L = u.shape[0], u.shape[1]
    n_grp = n_seq // SUB
    tl = 32
    nt = L // tl
    ncol = 2 * SSM_N
    fwd = pl.BlockSpec((SUB, tl, SSM_W), lambda g, i: (g, i, 0))
    bwd = pl.BlockSpec((SUB, tl, SSM_W), lambda g, i: (g, nt - 1 - i, 0))
    st = pl.BlockSpec((None, 2, SUB, ncol), lambda g, i: (g, 0, 0, 0))
    tile = lambda w: pltpu.VMEM((tl * SUB, w), F32)
    relayout = pltpu.VMEM((SSM_BLK, tl * SUB, SUB * SSM_GC), F32)
    return pl.pallas_call(
        _s5_kernel,
        grid=(n_grp, nt),
        in_specs=[fwd, bwd, _const_spec(bd.shape), _const_spec(cm.shape), _const_spec(lam.shape), st],
        out_specs=[fwd, bwd, st],
        out_shape=[jax.ShapeDtypeStruct(u.shape, F32), jax.ShapeDtypeStruct(u.shape, F32),
                   jax.ShapeDtypeStruct((n_grp, 2, SUB, ncol), F32)],
        scratch_shapes=[tile(ncol), tile(ncol), relayout, relayout, relayout, relayout,
                        pltpu.VMEM((2, SUB, ncol), F32)],
        compiler_params=_cparams("arbitrary", "arbitrary"),
        name="s5_scan",
    )(u, u, bd, cm, lam, h0)


def _state_to_cols(h_re, h_im):
    b = h_re.shape[0]
    r = h_re.reshape(b, 2, SSM_BLK, SUB * SSM_P)
    m = h_im.reshape(b, 2, SSM_BLK, SUB * SSM_P)
    return jnp.stack([r, m], axis=3).reshape(b, 2, 2 * SSM_N)


def _cols_to_state(cols):
    b = cols.shape[0]
    c = cols.reshape(b, 2, SSM_BLK, 2, SUB * SSM_P)
    return (c[:, :, :, 0].reshape(b, 2, SSM_G, SSM_P), c[:, :, :, 1].reshape(b, 2, SSM_G, SSM_P))


def _pool_kernel(sp_ref, o_ref, pad_s):
    L = o_ref.shape[0]
    pad_s[0:PAD, :] = jnp.zeros((PAD, POOL_W), F32)
    pad_s[PAD + L:PAD + L + PAD, :] = jnp.zeros((PAD, POOL_W), F32)
    pad_s[PAD:PAD + L, :] = sp_ref[...]
    rc = 256
    for gi, w in enumerate(POOL_WINDOWS):
        cols = slice(gi * POOL_G, (gi + 1) * POOL_G)
        for r0 in range(0, L, rc):
            acc = pad_s[PAD + r0 - w // 2:PAD + r0 - w // 2 + rc, cols]
            for dlt in range(-(w // 2) + 1, w // 2):
                acc = acc + pad_s[PAD + r0 + dlt:PAD + r0 + dlt + rc, cols]
            t = r0 + lax.broadcasted_iota(jnp.int32, (rc, POOL_G), 0)
            cnt = jnp.minimum(t + w // 2, L) - jnp.maximum(t - w // 2, 0)
            pooled = acc / cnt.astype(F32) - pad_s[PAD + r0:PAD + r0 + rc, cols]
            o_ref[r0:r0 + rc, cols] = pooled.astype(BF16)


def _pool(u_pool, n_seq, seq_len):
    L = seq_len
    return pl.pallas_call(
        _pool_kernel,
        grid=(n_seq,),
        in_specs=[pl.BlockSpec((L, POOL_W), lambda i: (i, 0))],
        out_specs=pl.BlockSpec((L, POOL_W), lambda i: (i, 0)),
        out_shape=jax.ShapeDtypeStruct((n_seq * L, POOL_W), BF16),
        scratch_shapes=[pltpu.VMEM((L + 2 * PAD, POOL_W), F32)],
        compiler_params=_cparams("arbitrary"),
        name="pool",
    )(u_pool)


def _merge_kernel(x_ref, odn_ref, yf_ref, yb_ref, sp_ref, pooled_ref, gate_ref, mod_ref,
                  sd_ref, gw_ref, gb_ref, pw_ref, ps_ref, wdn_ref, wssm_ref, wpool_ref, wout_ref, o_ref):
    y = yf_ref[...] + yb_ref[...] + sd_ref[...] * sp_ref[...]
    y = jax.nn.gelu(y)
    o_ssm = y * jax.nn.sigmoid(_mm(y, gw_ref[...]) + gb_ref[...])
    o_pool = jnp.concatenate(
        [_mm(pooled_ref[:, gi * POOL_G:(gi + 1) * POOL_G], pw_ref[gi]) for gi in range(len(POOL_WINDOWS))],
        axis=1) * ps_ref[...]
    gate = lambda i: jax.nn.sigmoid(gate_ref[:, i * D:(i + 1) * D].astype(F32))
    merged = (gate(0) * _mm(odn_ref[...], wdn_ref[...]) + gate(1) * _mm(o_ssm, wssm_ref[...])
              + gate(2) * _mm(o_pool, wpool_ref[...]))
    o_ref[...] = x_ref[...] + mod_ref[0, 2:3, :] * _mm(merged, wout_ref[...])


def _merge(x2, odn, yf, yb, u_ssm, pooled, gates, mod, seq_len, ssm_d, glu_w, glu_b, pool_w, pool_scale,
           wdn, wssm, wpool, wout):
    t = x2.shape[0]
    tm = TOKEN_TILE
    mod_map = _mod_map(mod, seq_len, tm)
    row = lambda w: pl.BlockSpec((tm, w), lambda i: (i, 0))
    return pl.pallas_call(
        _merge_kernel,
        grid=(t // tm,),
        in_specs=[row(D), row(DNW), row(SSM_W), row(SSM_W), row(SSM_W), row(POOL_W), row(3 * D),
                  pl.BlockSpec((1, 6, D), mod_map),
                  _const_spec((1, SSM_W)), _const_spec(glu_w.shape), _const_spec((1, SSM_W)),
                  _const_spec(pool_w.shape), _const_spec((1, POOL_W)),
                  _const_spec(wdn.shape), _const_spec(wssm.shape), _const_spec(wpool.shape),
                  _const_spec(wout.shape)],
        out_specs=row(D),
        out_shape=jax.ShapeDtypeStruct((t, D), F32),
        compiler_params=_cparams("arbitrary"),
        name="merge",
    )(x2, odn, yf, yb, u_ssm, pooled, gates, mod, ssm_d.reshape(1, SSM_W), glu_w, glu_b.reshape(1, SSM_W),
      pool_w, pool_scale.reshape(1, POOL_W), wdn, wssm, wpool, wout)


def _ffn_kernel(n_seq, final, x_ref, mod_ref, g_ref, wup_ref, cw_ref, wdn_ref, fg_ref, o_ref,
                h_s, pad0_s, pad1_s, act0_s, act1_s, acc_s):
    rows = x_ref.shape[0]
    L = rows // n_seq
    tm = 256
    pad_s = (pad0_s, pad1_s)
    act_s = (act0_s, act1_s)

    def norm_tile(it, carry):
        rs = pl.ds(pl.multiple_of(it * tm, tm), tm)
        y = _rms(x_ref[rs, :]) * g_ref[...]
        h_s[rs, :] = (y * (1.0 + mod_ref[0, 4:5, :]) + mod_ref[0, 3:4, :]).astype(BF16)
        return carry

    lax.fori_loop(0, rows // tm, norm_tile, 0)

    for slot in range(2):
        for n in range(n_seq):
            pad_s[slot][n, 0:PAD, :] = jnp.zeros((PAD, 2 * FF_BLK), F32)
            pad_s[slot][n, PAD + L:PAD + L + PAD, :] = jnp.zeros((PAD, 2 * FF_BLK), F32)
    acc_s[...] = jnp.zeros(acc_s.shape, F32)
    rc = min(L, 256)

    def gate_cols(j):
        return pl.ds(pl.multiple_of(j * FF_BLK, FF_BLK), FF_BLK)

    def val_cols(j):
        return pl.ds(pl.multiple_of(DFF + j * FF_BLK, 128), FF_BLK)

    def up(j, slot):
        h = h_s[...]
        up_g = jnp.dot(h, wup_ref[:, gate_cols(j)], preferred_element_type=F32)
        up_v = jnp.dot(h, wup_ref[:, val_cols(j)], preferred_element_type=F32)
        for n in range(n_seq):
            pad_s[slot][n, PAD:PAD + L, 0:FF_BLK] = up_g[n * L:(n + 1) * L]
            pad_s[slot][n, PAD:PAD + L, FF_BLK:2 * FF_BLK] = up_v[n * L:(n + 1) * L]

    def conv_act(j, slot):
        wg = cw_ref[:, gate_cols(j)]
        wv = cw_ref[:, val_cols(j)]
        for n in range(n_seq):
            for r0 in range(0, L, rc):
                def conv(w, cols):
                    return (w[0:1] * pad_s[slot][n, PAD - 1 + r0:PAD - 1 + r0 + rc, cols]
                            + w[1:2] * pad_s[slot][n, PAD + r0:PAD + r0 + rc, cols]
                            + w[2:3] * pad_s[slot][n, PAD + 1 + r0:PAD + 1 + r0 + rc, cols])
                hg = conv(wg, slice(0, FF_BLK))
                hv = conv(wv, slice(FF_BLK, 2 * FF_BLK))
                act_s[slot][n * L + r0:n * L + r0 + rc, :] = (_silu(hg) * hv).astype(BF16)

    def down(j, slot):
        acc_s[...] += jnp.dot(act_s[slot][...], wdn_ref[gate_cols(j), :], preferred_element_type=F32)

    n_blk = DFF // FF_BLK
    assert n_blk % 2 == 1
    up(0, 0)

    def col_pair(jj, carry):
        j = 2 * jj
        up(j + 1, 1)
        conv_act(j, 0)
        down(j, 0)
        up(j + 2, 0)
        conv_act(j + 1, 1)
        down(j + 1, 1)
        return carry

    lax.fori_loop(0, n_blk // 2, col_pair, 0)
    conv_act(n_blk - 1, 0)
    down(n_blk - 1, 0)

    def out_tile(it, carry):
        rs = pl.ds(pl.multiple_of(it * tm, tm), tm)
        y = x_ref[rs, :] + mod_ref[0, 5:6, :] * acc_s[rs, :]
        if final:
            y = _rms(y) * fg_ref[...]
        o_ref[rs, :] = y
        return carry

    lax.fori_loop(0, rows // tm, out_tile, 0)


def _ffn(x2, mod, n_tile_seq, seq_len, norm_g, wup, conv_w, wdn, final_g, final):
    t = x2.shape[0]
    rows = n_tile_seq * seq_len
    per_seq = mod.shape[0] > 1
    mod_map = (lambda i: (i, 0, 0)) if per_seq else (lambda i: (0, 0, 0))
    if per_seq:
        assert n_tile_seq == 1
    row = pl.BlockSpec((rows, D), lambda i: (i, 0))
    return pl.pallas_call(
        functools.partial(_ffn_kernel, n_tile_seq, final),
        grid=(t // rows,),
        in_specs=[row, pl.BlockSpec((1, 6, D), mod_map), _const_spec((1, D)), _const_spec(wup.shape),
                  _const_spec(conv_w.shape), _const_spec(wdn.shape), _const_spec((1, D))],
        out_specs=row,
        out_shape=jax.ShapeDtypeStruct((t, D), F32),
        scratch_shapes=[pltpu.VMEM((rows, D), BF16),
                        pltpu.VMEM((n_tile_seq, seq_len + 2 * PAD, 2 * FF_BLK), F32),
                        pltpu.VMEM((n_tile_seq, seq_len + 2 * PAD, 2 * FF_BLK), F32),
                        pltpu.VMEM((rows, FF_BLK), BF16), pltpu.VMEM((rows, FF_BLK), BF16),
                        pltpu.VMEM((rows, D), F32)],
        compiler_params=_cparams("arbitrary"),
        name="conv_ffn",
    )(x2, mod, norm_g.reshape(1, D), wup, conv_w, wdn, final_g.reshape(1, D))


def _grid_pos_embed(rows, dim):
    quarter = dim // 4
    omega = 1.0 / (10000.0 ** (jnp.arange(quarter, dtype=F32) / quarter))
    r = jnp.broadcast_to(jnp.arange(rows, dtype=F32)[:, None], (rows, GRID_W)).reshape(-1)
    col = jnp.broadcast_to(jnp.arange(GRID_W, dtype=F32)[None, :], (rows, GRID_W)).reshape(-1)

    def sincos(p):
        ang = p[:, None] * omega[None, :]
        return jnp.concatenate([jnp.sin(ang), jnp.cos(ang)], axis=-1)

    return jnp.concatenate([sincos(r), sincos(col)], axis=-1)


def _layer(x2, mod, n_seq, seq_len, n_tile_seq, s_dn, h_cols, p, final):
    qkvz, ab, u_ssm, u_pool, gates = _in_proj(x2, mod, seq_len, p["norm1_g"], p["w_in"])
    odn, st_dn = _deltanet(qkvz, ab, n_seq, seq_len, p["dn_conv"], p["dn_a_log"], p["dn_dt_bias"],
                           p["dn_norm_g"], s_dn)
    h0 = h_cols.reshape(n_seq // SUB, SUB, 2, 2 * SSM_N).transpose(0, 2, 1, 3)
    yf, yb, fin = _s5_scan(u_ssm.reshape(n_seq, seq_len, SSM_W), p["bd"], p["cm"], p["lam"], h0)
    yf = yf.reshape(n_seq * seq_len, SSM_W)
    yb = yb.reshape(n_seq * seq_len, SSM_W)
    fin = fin.transpose(0, 2, 1, 3).reshape(n_seq, 2, 2 * SSM_N)
    pooled = _pool(u_pool, n_seq, seq_len)
    x2 = _merge(x2, odn, yf, yb, u_ssm, pooled, gates, mod, seq_len, p["ssm_d"], p["glu_w"], p["ssm_glu_b"],
                p["pool_w"], p["pool_scale"], p["wdn"], p["wssm"], p["wpool"], p["wout"])
    x2 = _ffn(x2, mod, n_tile_seq, seq_len, p["norm2_g"], p["wup"], p["ffn_conv"], p["wdown"],
              p["final_norm_g"], final)
    return x2, st_dn, fin


def kernel(x_prompt, x_sample, state_dn, state_ssm_re, state_ssm_im, c, c_ctx, norm1_g, norm2_g, w_ada, b_ada, w_in, dn_conv, dn_a_log, dn_dt_bias, dn_norm_g, ssm_lambda_re, ssm_lambda_im, ssm_log_step, ssm_b_re, ssm_b_im, ssm_c_re, ssm_c_im, ssm_d, ssm_glu_w, ssm_glu_b, pool_w, pool_scale, w_branch_dn, w_branch_ssm, w_branch_pool, w_out, ffn_w_up, ffn_conv, ffn_w_down, final_norm_g):
    bc, lc, _ = x_prompt.shape
    bl, ll, _ = x_sample.shape

    n_cond = 16
    conds = jnp.zeros((n_cond, D), F32).at[0].set(c_ctx).at[1:1 + bl].set(c)
    mods = _ada(conds, w_ada, b_ada).reshape(DEPTH, n_cond, 6, D)

    x_ctx = x_prompt.reshape(bc * lc, D)
    x_lat = _add_pos(x_sample, _grid_pos_embed(ll // GRID_W, D)).reshape(bl * ll, D)

    zero_dn = jnp.zeros((bc, 2, HEADS, HD, HD), F32)
    zero_h = jnp.zeros((bc, 2, 2 * SSM_N), F32)
    new_dn, new_re, new_im = [], [], []
    for l in range(DEPTH):
        lam, bb = _s5_discretise(ssm_lambda_re[l], ssm_lambda_im[l], ssm_log_step[l], ssm_b_re[l], ssm_b_im[l])
        bd, cm = _s5_block_mats(bb, ssm_c_re[l], ssm_c_im[l])
        p = dict(
            norm1_g=norm1_g[l], norm2_g=norm2_g[l], final_norm_g=final_norm_g, w_in=_pack_w_in(w_in[l]),
            dn_conv=dn_conv[l], dn_a_log=dn_a_log[l], dn_dt_bias=dn_dt_bias[l], dn_norm_g=dn_norm_g[l],
            bd=bd, cm=cm, lam=lam, ssm_d=ssm_d[l], glu_w=ssm_glu_w[l].astype(BF16), ssm_glu_b=ssm_glu_b[l],
            pool_w=pool_w[l].astype(BF16), pool_scale=pool_scale[l],
            wdn=w_branch_dn[l].astype(BF16), wssm=w_branch_ssm[l].astype(BF16),
            wpool=w_branch_pool[l].astype(BF16), wout=w_out[l].astype(BF16),
            wup=ffn_w_up[l].astype(BF16), ffn_conv=ffn_conv[l], wdown=ffn_w_down[l].astype(BF16),
        )
        final = l == DEPTH - 1
        x_ctx, st_dn, fin = _layer(x_ctx, mods[l, 0:1], bc, lc, 4, zero_dn, zero_h, p, final)
        new_dn.append(st_dn)
        f_re, f_im = _cols_to_state(fin)
        new_re.append(f_re)
        new_im.append(f_im)
        h_lat = _state_to_cols(state_ssm_re[:, l], state_ssm_im[:, l])
        x_lat, _, _ = _layer(x_lat, mods[l, 1:1 + bl], bl, ll, 1, state_dn[:, l], h_lat, p, final)

    return (x_ctx.reshape(bc, lc, D), x_lat.reshape(bl, ll, D), jnp.stack(new_dn, axis=1),
            jnp.stack(new_re, axis=1), jnp.stack(new_im, axis=1))
```

```python
import functools

import jax
import jax.numpy as jnp
from jax import lax
from jax.experimental import pallas as pl
from jax.experimental.pallas import tpu as pltpu

F32 = jnp.float32
BF16 = jnp.bfloat16

D = 1024
DEPTH = 2
GRID_W = 64
EPS = 1e-6
HEADS = 4
HD = 128
DNW = HEADS * HD
CH = 128
N_LVL = 7
SSM_W = 512
SSM_G = 32
SSM_GC = 16
SSM_P = 64
SSM_N = SSM_G * SSM_P
SSM_BLK = 4
POOL_WINDOWS = (2, 4, 8, 16)
POOL_W = 512
POOL_G = 128
DFF = 2816
FF_BLK = 256
PAD = 8
SUB = 8
VMEM_LIMIT = 56 << 20


def _cparams(*sem):
    return pltpu.CompilerParams(dimension_semantics=sem, vmem_limit_bytes=VMEM_LIMIT)


def _const_spec(shape):
    zeros = (0,) * len(shape)
    return pl.BlockSpec(shape, lambda *_: zeros, pipeline_mode=pl.Buffered(1))


def _mm(a, b):
    return jnp.dot(a.astype(BF16), b.astype(BF16), preferred_element_type=F32)


def _mm_nt(a, b):
    return lax.dot_general(a.astype(BF16), b.astype(BF16), (((1,), (1,)), ((), ())),
                           preferred_element_type=F32)


def _mm_tn(a, b):
    return lax.dot_general(a.astype(BF16), b.astype(BF16), (((0,), (0,)), ((), ())),
                           preferred_element_type=F32)


def _mm_exact(a, b):
    return jnp.dot(a, b, precision=lax.Precision.HIGHEST, preferred_element_type=F32)


def _silu(x):
    return x * jax.nn.sigmoid(x)


def _rms(x):
    return x * lax.rsqrt(jnp.mean(x * x, axis=-1, keepdims=True) + EPS)


def _ada_kernel(c_ref, w_ref, b_ref, o_ref):
    o_ref[...] = _mm(_silu(c_ref[...]), w_ref[...]) + b_ref[...]


def _ada(conds, w_ada, b_ada):
    n = conds.shape[0]
    tn = 1536
    return pl.pallas_call(
        _ada_kernel,
        grid=(DEPTH, 6 * D // tn),
        in_specs=[pl.BlockSpec((n, D), lambda l, j: (0, 0)),
                  pl.BlockSpec((None, D, tn), lambda l, j: (l, 0, j)),
                  pl.BlockSpec((None, 1, tn), lambda l, j: (l, 0, j))],
        out_specs=pl.BlockSpec((None, n, tn), lambda l, j: (l, 0, j)),
        out_shape=jax.ShapeDtypeStruct((DEPTH, n, 6 * D), F32),
        compiler_params=_cparams("arbitrary", "arbitrary"),
        name="ada",
    )(conds, w_ada, b_ada.reshape(DEPTH, 1, 6 * D))


def _addpos_kernel(x_ref, p_ref, o_ref):
    o_ref[...] = x_ref[...] + p_ref[...]


def _add_pos(x, pos):
    b, l, _ = x.shape
    return pl.pallas_call(
        _addpos_kernel,
        grid=(b,),
        in_specs=[pl.BlockSpec((None, l, D), lambda i: (i, 0, 0)), _const_spec((l, D))],
        out_specs=pl.BlockSpec((None, l, D), lambda i: (i, 0, 0)),
        out_shape=jax.ShapeDtypeStruct(x.shape, F32),
        compiler_params=_cparams("arbitrary"),
        name="add_pos",
    )(x, pos)


IN_AB = 4 * DNW
IN_SSM = IN_AB + 128
IN_POOL = IN_SSM + SSM_W
IN_GATE = IN_POOL + POOL_W
IN_END = IN_GATE + 3 * D


def _inproj_kernel(x_ref, mod_ref, g_ref, w_ref, qkvz_ref, ab_ref, ussm_ref, upool_ref, gate_ref):
    y = _rms(x_ref[...]) * g_ref[...]
    h = (y * (1.0 + mod_ref[0, 1:2, :]) + mod_ref[0, 0:1, :]).astype(BF16)
    qkvz_ref[...] = jnp.dot(h, w_ref[:, 0:IN_AB], preferred_element_type=F32)
    ab_ref[...] = jnp.dot(h, w_ref[:, IN_AB:IN_SSM], preferred_element_type=F32)
    ussm_ref[...] = jnp.dot(h, w_ref[:, IN_SSM:IN_POOL], preferred_element_type=F32)
    upool_ref[...] = jnp.dot(h, w_ref[:, IN_POOL:IN_GATE], preferred_element_type=F32)
    gate_ref[...] = jnp.dot(h, w_ref[:, IN_GATE:IN_END], preferred_element_type=F32).astype(BF16)


TOKEN_TILE = 512


def _pack_w_in(w_in):
    n_ab = 4 * HEADS
    return jnp.concatenate([w_in[:, :IN_AB], w_in[:, IN_AB:IN_AB + n_ab], jnp.zeros((D, 128 - n_ab), F32),
                            w_in[:, IN_AB + n_ab:]], axis=1).astype(BF16)


def _mod_map(mod, seq_len, tm):
    if mod.shape[0] == 1:
        return lambda i: (0, 0, 0)
    assert seq_len % tm == 0
    return lambda i: (i // (seq_len // tm), 0, 0)


def _in_proj(x2, mod, seq_len, norm_g, w):
    t = x2.shape[0]
    tm = TOKEN_TILE
    mod_map = _mod_map(mod, seq_len, tm)
    row = lambda w: pl.BlockSpec((tm, w), lambda i: (i, 0))
    return pl.pallas_call(
        _inproj_kernel,
        grid=(t // tm,),
        in_specs=[row(D), pl.BlockSpec((1, 6, D), mod_map), _const_spec((1, D)), _const_spec(w.shape)],
        out_specs=[row(4 * DNW), row(128), row(SSM_W), row(POOL_W), row(3 * D)],
        out_shape=[jax.ShapeDtypeStruct((t, 4 * DNW), F32), jax.ShapeDtypeStruct((t, 128), F32),
                   jax.ShapeDtypeStruct((t, SSM_W), F32),
                   jax.ShapeDtypeStruct((t, POOL_W), F32),
                   jax.ShapeDtypeStruct((t, 3 * D), BF16)],
        compiler_params=_cparams("arbitrary"),
        name="in_proj",
    )(x2, mod, norm_g.reshape(1, D), w)


def _dn_kernel(qkv_ref, z_ref, ab_ref, cw_ref, alog_ref, dtb_ref, ng_ref, s0_ref,
               o_ref, sfin_ref,
               pad_s, q_s, k_s, v_s, gc_s, gr_s, b_s, o_s, st_s, lm_s):
    L = o_ref.shape[0]
    nc = L // CH

    ab = ab_ref[...]
    xs = ab + dtb_ref[...]
    softplus = jnp.maximum(xs, 0.0) + jnp.log1p(jnp.exp(-jnp.abs(xs)))
    g = -jnp.exp(alog_ref[...]) * softplus
    gt = g.T
    b_s[...] = jax.nn.sigmoid(ab)
    ri = lax.broadcasted_iota(jnp.int32, (CH, CH), 0)
    ci = lax.broadcasted_iota(jnp.int32, (CH, CH), 1)
    lo = (ri >= ci).astype(F32)
    up = (ri <= ci).astype(F32)
    for c in range(nc):
        r = slice(c * CH, (c + 1) * CH)
        for d, (tri, tri_t) in enumerate(((lo, up), (up, lo))):
            gc_s[d, r, :] = _mm_exact(tri, g[r, :])
            gr_s[d, :, r] = _mm_exact(gt[:, r], tri_t)

    pad_s[0:PAD, :] = jnp.zeros((PAD, 3 * DNW), F32)
    pad_s[PAD + L:PAD + L + PAD, :] = jnp.zeros((PAD, 3 * DNW), F32)
    pad_s[PAD:PAD + L, :] = qkv_ref[...]
    rc = 256
    for cb in range(3 * HEADS):
        cols = slice(cb * HD, (cb + 1) * HD)
        w = cw_ref[:, cols]
        for r0 in range(0, L, rc):
            y = (w[0:1] * pad_s[PAD - 1 + r0:PAD - 1 + r0 + rc, cols]
                 + w[1:2] * pad_s[PAD + r0:PAD + r0 + rc, cols]
                 + w[2:3] * pad_s[PAD + 1 + r0:PAD + 1 + r0 + rc, cols])
            y = _silu(y)
            h = cb % HEADS
            hc = slice(h * HD, (h + 1) * HD)
            if cb < HEADS:
                y = y * lax.rsqrt(jnp.sum(y * y, axis=-1, keepdims=True) + EPS) * (HD ** -0.5)
                q_s[r0:r0 + rc, hc] = y
            elif cb < 2 * HEADS:
                y = y * lax.rsqrt(jnp.sum(y * y, axis=-1, keepdims=True) + EPS)
                k_s[r0:r0 + rc, hc] = y
            else:
                v_s[r0:r0 + rc, hc] = y

    for d in range(2):
        for h in range(HEADS):
            st_s[d * HEADS + h] = s0_ref[d, h]
    o_s[...] = jnp.zeros(o_s.shape, F32)

    eye = (ri == ci).astype(F32)
    for lvl in range(N_LVL):
        pair = ((ri >> (lvl + 1)) == (ci >> (lvl + 1))) & ((ri >> lvl) != (ci >> lvl))
        lm_s[lvl] = pair.astype(F32)

    def chunk_steps(cs):
        incl = [ri >= ci, ri <= ci]
        strict = [ri > ci, ri < ci]
        last = [CH - 1, 0]
        hcs = [slice(h * HD, (h + 1) * HD) for h in range(HEADS)]
        rows, gcol, grow = {}, {}, {}
        for n, c in enumerate(cs):
            for d in range(2):
                cidx = c if d == 0 else nc - 1 - c
                r = pl.ds(pl.multiple_of(cidx * CH, CH), CH)
                rows[n, d] = r
                gcol[n, d] = gc_s[d, r, :]
                grow[n, d] = gr_s[d, :, r]
        keys = [(n, d, h) for n in range(len(cs)) for d in range(2) for h in range(HEADS)]
        q, k, v, gc, gtot, beta, decay, eg, kb = ({} for _ in range(9))
        for key in keys:
            n, d, h = key
            s = d * HEADS + h
            r = rows[n, d]
            q[key] = q_s[r, hcs[h]]
            k[key] = k_s[r, hcs[h]]
            v[key] = v_s[r, hcs[h]]
            gc[key] = gcol[n, d][:, s:s + 1]
            gtot[key] = gcol[n, d][last[d]:last[d] + 1, s:s + 1]
            beta[key] = b_s[r, 2 * HEADS + s:2 * HEADS + s + 1]
            decay[key] = jnp.where(incl[d], jnp.exp(gc[key] - grow[n, d][s:s + 1, :]), 0.0)
            eg[key] = jnp.exp(gc[key])
            kb[key] = k[key] * beta[key]
        raw = {key: _mm_nt(jnp.concatenate([kb[key], q[key]], axis=0), k[key]) for key in keys}
        a = {key: jnp.where(strict[key[1]], raw[key][:CH] * decay[key], 0.0) for key in keys}
        qk = {key: raw[key][CH:] * decay[key] for key in keys}
        t = {key: eye - a[key] * lm_s[0] for key in keys}
        for lvl in range(1, N_LVL):
            x = {key: _mm(a[key] * lm_s[lvl], t[key]) for key in keys}
            t = {key: t[key] - _mm(t[key], x[key]) for key in keys}
        uw = {key: _mm(t[key], jnp.concatenate([v[key] * beta[key], kb[key] * eg[key]], axis=1)) for key in keys}
        for n in range(len(cs)):
            step = [key for key in keys if key[0] == n]
            st = {key: st_s[key[1] * HEADS + key[2]] for key in step}
            ws = {key: _mm(jnp.concatenate([uw[key][:, HD:], q[key] * eg[key]], axis=0), st[key]) for key in step}
            v_new = {key: uw[key][:, :HD] - ws[key][:CH] for key in step}
            o = {key: ws[key][CH:] + _mm(qk[key], v_new[key]) for key in step}
            for key in step:
                st_s[key[1] * HEADS + key[2]] = (st[key] * jnp.exp(gtot[key])
                                                 + _mm_tn(k[key] * jnp.exp(gtot[key] - gc[key]), v_new[key]))
            for key in step:
                r, hc = rows[n, key[1]], hcs[key[2]]
                o_s[r, hc] = o_s[r, hc] + o[key]

    steps_per_iter = 2
    assert nc % steps_per_iter == 0

    def chunk_iter(it, carry):
        chunk_steps([it * steps_per_iter + n for n in range(steps_per_iter)])
        return carry

    lax.fori_loop(0, nc // steps_per_iter, chunk_iter, 0)

    for d in range(2):
        for h in range(HEADS):
            sfin_ref[d, h] = st_s[d * HEADS + h]
    for h in range(HEADS):
        hc = slice(h * HD, (h + 1) * HD)
        for r0 in range(0, L, rc):
            z = z_ref[r0:r0 + rc, hc]
            o_ref[r0:r0 + rc, hc] = (_rms(o_s[r0:r0 + rc, hc]) * ng_ref[...] * _silu(z)).astype(BF16)


def _deltanet(qkvz, ab, n_seq, seq_len, conv_w, a_log, dt_bias, norm_g, s0):
    L = seq_len
    pad_row = lambda v: jnp.zeros((1, 128), F32).at[0, :2 * HEADS].set(v.reshape(-1))
    st_spec = pl.BlockSpec((None, 2, HEADS, HD, HD), lambda i: (i, 0, 0, 0, 0))
    return pl.pallas_call(
        _dn_kernel,
        grid=(n_seq,),
        in_specs=[pl.BlockSpec((L, 3 * DNW), lambda i: (i, 0)),
                  pl.BlockSpec((L, DNW), lambda i: (i, 3)),
                  pl.BlockSpec((L, 128), lambda i: (i, 0)),
                  _const_spec((3, 3 * DNW)), _const_spec((1, 128)), _const_spec((1, 128)),
                  _const_spec((1, HD)), st_spec],
        out_specs=[pl.BlockSpec((L, DNW), lambda i: (i, 0)), st_spec],
        out_shape=[jax.ShapeDtypeStruct((n_seq * L, DNW), BF16),
                   jax.ShapeDtypeStruct((n_seq, 2, HEADS, HD, HD), F32)],
        scratch_shapes=[pltpu.VMEM((L + 2 * PAD, 3 * DNW), F32),
                        pltpu.VMEM((L, DNW), F32), pltpu.VMEM((L, DNW), F32), pltpu.VMEM((L, DNW), F32),
                        pltpu.VMEM((2, L, 128), F32), pltpu.VMEM((2, 128, L), F32), pltpu.VMEM((L, 128), F32),
                        pltpu.VMEM((L, DNW), F32), pltpu.VMEM((2 * HEADS, HD, HD), F32),
                        pltpu.VMEM((N_LVL, CH, CH), F32)],
        compiler_params=_cparams("arbitrary"),
        name="deltanet",
    )(qkvz, qkvz, ab, conv_w, pad_row(a_log), pad_row(dt_bias), norm_g.reshape(1, HD), s0)


def _s5disc_kernel(lre_ref, lim_ref, ls_ref, bre_ref, bim_ref, lam_ref, bb_ref):
    lam_re = lre_ref[...]
    lam_im = lim_ref[...]
    step = jnp.exp(ls_ref[...])
    mag = jnp.exp(lam_re * step)
    ang = lam_im * step
    lb_re = mag * jnp.cos(ang)
    lb_im = mag * jnp.sin(ang)
    nr, ni = lb_re - 1.0, lb_im
    den = lam_re * lam_re + lam_im * lam_im
    f_re = (nr * lam_re + ni * lam_im) / den
    f_im = (ni * lam_re - nr * lam_im) / den
    lam_ref[:, 0, :] = lb_re
    lam_ref[:, 1, :] = lb_im
    b_re = bre_ref[...]
    b_im = bim_ref[...]
    for d in range(2):
        fr = f_re[d:d + 1]
        fi = f_im[d:d + 1]
        bb_ref[d, 0] = fr * b_re - fi * b_im
        bb_ref[d, 1] = fr * b_im + fi * b_re


def _s5_discretise(lam_re, lam_im, log_step, b_re, b_im):
    ls = jnp.repeat(log_step, SSM_P, axis=1)
    bt = lambda b: b.reshape(SSM_N, SSM_GC).T
    return pl.pallas_call(
        _s5disc_kernel,
        out_shape=[jax.ShapeDtypeStruct((2, 2, SSM_N), F32),
                   jax.ShapeDtypeStruct((2, 2, SSM_GC, SSM_N), F32)],
        name="s5_discretise",
    )(lam_re.reshape(2, SSM_N), lam_im.reshape(2, SSM_N), ls, bt(b_re), bt(b_im))


def _s5_block_mats(bb, c_re, c_im):
    eye = jnp.eye(SUB, dtype=F32)
    b6 = bb.reshape(2, 2, SSM_GC, SSM_BLK, SUB, SSM_P)
    bd = jnp.einsum("drcjgp,gh->djgcrhp", b6, eye).reshape(2, SSM_BLK, SUB * SSM_GC, 2 * SUB * SSM_P)
    c2 = jnp.stack([c_re, -c_im], axis=0).reshape(2, SSM_BLK, SUB, SSM_GC, SSM_P)
    cm = jnp.einsum("rjgcp,gh->jrgphc", c2, eye).reshape(SSM_BLK, 2 * SUB * SSM_P, SUB * SSM_GC)
    return bd.astype(BF16), cm.astype(BF16)


def _s5_kernel(uf_ref, ub_ref, bd_ref, cm_ref, lam_ref, h0_ref, yf_ref, yb_ref, fin_ref,
               xf_s, xb_s, uf_s, ub_s, yf_s, yb_s, car_s):
    i = pl.program_id(1)
    tl = uf_ref.shape[1]
    rows = tl * SUB
    bw = 2 * SUB * SSM_P
    half = SUB * SSM_P

    @pl.when(i == 0)
    def _():
        car_s[...] = h0_ref[...]

    def seq_rows(s):
        return pl.ds(s, tl, stride=SUB)

    blk_w = SUB * SSM_GC

    def expand(d, u_ref, ut_s, x_s):
        for s in range(SUB):
            for j in range(SSM_BLK):
                ut_s[j, seq_rows(s), :] = u_ref[s, :, j * blk_w:(j + 1) * blk_w]
        for j in range(SSM_BLK):
            x_s[:, j * bw:(j + 1) * bw] = jnp.dot(ut_s[j].astype(BF16), bd_ref[d, j],
                                                  preferred_element_type=F32)

    def scan(d, x_s):
        lw = 256
        for j in range(SSM_BLK):
            for kk in range(half // lw):
                cre = slice(j * bw + kk * lw, j * bw + (kk + 1) * lw)
                cim = slice(j * bw + half + kk * lw, j * bw + half + (kk + 1) * lw)
                n0 = j * half + kk * lw
                lr = jnp.broadcast_to(lam_ref[d, 0:1, n0:n0 + lw], (SUB, lw))
                li = jnp.broadcast_to(lam_ref[d, 1:2, n0:n0 + lw], (SUB, lw))
                xr = car_s[d, :, cre]
                xi = car_s[d, :, cim]
                for t in (range(tl) if d == 0 else range(tl - 1, -1, -1)):
                    r = slice(t * SUB, (t + 1) * SUB)
                    xr, xi = lr * xr - li * xi + x_s[r, cre], lr * xi + li * xr + x_s[r, cim]
                    x_s[r, cre] = xr
                    x_s[r, cim] = xi
                car_s[d, :, cre] = xr
                car_s[d, :, cim] = xi

    def project(x_s, yt_s, y_ref):
        for j in range(SSM_BLK):
            yt_s[j] = jnp.dot(x_s[:, j * bw:(j + 1) * bw].astype(BF16), cm_ref[j], preferred_element_type=F32)
        for s in range(SUB):
            for j in range(SSM_BLK):
                y_ref[s, :, j * blk_w:(j + 1) * blk_w] = yt_s[j, seq_rows(s), :]

    expand(0, uf_ref, uf_s, xf_s)
    scan(0, xf_s)
    expand(1, ub_ref, ub_s, xb_s)
    project(xf_s, yf_s, yf_ref)
    scan(1, xb_s)
    project(xb_s, yb_s, yb_ref)

    @pl.when(i == pl.num_programs(1) - 1)
    def _():
        fin_ref[...] = car_s[...]


def _s5_scan(u, bd, cm, lam, h0):
    n_seq, L = u.shape[0], u.shape[1]
    n_grp = n_seq // SUB
    tl = 32
    nt = L // tl
    ncol = 2 * SSM_N
    fwd = pl.BlockSpec((SUB, tl, SSM_W), lambda g, i: (g, i, 0))
    bwd = pl.BlockSpec((SUB, tl, SSM_W), lambda g, i: (g, nt - 1 - i, 0))
    st = pl.BlockSpec((None, 2, SUB, ncol), lambda g, i: (g, 0, 0, 0))
    tile = lambda w: pltpu.VMEM((tl * SUB, w), F32)
    relayout = pltpu.VMEM((SSM_BLK, tl * SUB, SUB * SSM_GC), F32)
    return pl.pallas_call(
        _s5_kernel,
        grid=(n_grp, nt),
        in_specs=[fwd, bwd, _const_spec(bd.shape), _const_spec(cm.shape), _const_spec(lam.shape), st],
        out_specs=[fwd, bwd, st],
        out_shape=[jax.ShapeDtypeStruct(u.shape, F32), jax.ShapeDtypeStruct(u.shape, F32),
                   jax.ShapeDtypeStruct((n_grp, 2, SUB, ncol), F32)],
        scratch_shapes=[tile(ncol), tile(ncol), relayout, relayout, relayout, relayout,
                        pltpu.VMEM((2, SUB, ncol), F32)],
        compiler_params=_cparams("arbitrary", "arbitrary"),
        name="s5_scan",
    )(u, u, bd, cm, lam, h0)


def _state_to_cols(h_re, h_im):
    b = h_re.shape[0]
    r = h_re.reshape(b, 2, SSM_BLK, SUB * SSM_P)
    m = h_im.reshape(b, 2, SSM_BLK, SUB * SSM_P)
    return jnp.stack([r, m], axis=3).reshape(b, 2, 2 * SSM_N)


def _cols_to_state(cols):
    b = cols.shape[0]
    c = cols.reshape(b, 2, SSM_BLK, 2, SUB * SSM_P)
    return (c[:, :, :, 0].reshape(b, 2, SSM_G, SSM_P), c[:, :, :, 1].reshape(b, 2, SSM_G, SSM_P))


def _pool_kernel(sp_ref, o_ref, pad_s):
    L = o_ref.shape[0]
    pad_s[0:PAD, :] = jnp.zeros((PAD, POOL_W), F32)
    pad_s[PAD + L:PAD + L + PAD, :] = jnp.zeros((PAD, POOL_W), F32)
    pad_s[PAD:PAD + L, :] = sp_ref[...]
    rc = 256
    for gi, w in enumerate(POOL_WINDOWS):
        cols = slice(gi * POOL_G, (gi + 1) * POOL_G)
        for r0 in range(0, L, rc):
            acc = pad_s[PAD + r0 - w // 2:PAD + r0 - w // 2 + rc, cols]
            for dlt in range(-(w // 2) + 1, w // 2):
                acc = acc + pad_s[PAD + r0 + dlt:PAD + r0 + dlt + rc, cols]
            t = r0 + lax.broadcasted_iota(jnp.int32, (rc, POOL_G), 0)
            cnt = jnp.minimum(t + w // 2, L) - jnp.maximum(t - w // 2, 0)
            pooled = acc / cnt.astype(F32) - pad_s[PAD + r0:PAD + r0 + rc, cols]
            o_ref[r0:r0 + rc, cols] = pooled.astype(BF16)


def _pool(u_pool, n_seq, seq_len):
    L = seq_len
    return pl.pallas_call(
        _pool_kernel,
        grid=(n_seq,),
        in_specs=[pl.BlockSpec((L, POOL_W), lambda i: (i, 0))],
        out_specs=pl.BlockSpec((L, POOL_W), lambda i: (i, 0)),
        out_shape=jax.ShapeDtypeStruct((n_seq * L, POOL_W), BF16),
        scratch_shapes=[pltpu.VMEM((L + 2 * PAD, POOL_W), F32)],
        compiler_params=_cparams("arbitrary"),
        name="pool",
    )(u_pool)


def _merge_kernel(x_ref, odn_ref, yf_ref, yb_ref, sp_ref, pooled_ref, gate_ref, mod_ref,
                  sd_ref, gw_ref, gb_ref, pw_ref, ps_ref, wdn_ref, wssm_ref, wpool_ref, wout_ref, o_ref):
    y = yf_ref[...] + yb_ref[...] + sd_ref[...] * sp_ref[...]
    y = jax.nn.gelu(y)
    o_ssm = y * jax.nn.sigmoid(_mm(y, gw_ref[...]) + gb_ref[...])
    o_pool = jnp.concatenate(
        [_mm(pooled_ref[:, gi * POOL_G:(gi + 1) * POOL_G], pw_ref[gi]) for gi in range(len(POOL_WINDOWS))],
        axis=1) * ps_ref[...]
    gate = lambda i: jax.nn.sigmoid(gate_ref[:, i * D:(i + 1) * D].astype(F32))
    merged = (gate(0) * _mm(odn_ref[...], wdn_ref[...]) + gate(1) * _mm(o_ssm, wssm_ref[...])
              + gate(2) * _mm(o_pool, wpool_ref[...]))
    o_ref[...] = x_ref[...] + mod_ref[0, 2:3, :] * _mm(merged, wout_ref[...])


def _merge(x2, odn, yf, yb, u_ssm, pooled, gates, mod, seq_len, ssm_d, glu_w, glu_b, pool_w, pool_scale,
           wdn, wssm, wpool, wout):
    t = x2.shape[0]
    tm = TOKEN_TILE
    mod_map = _mod_map(mod, seq_len, tm)
    row = lambda w: pl.BlockSpec((tm, w), lambda i: (i, 0))
    return pl.pallas_call(
        _merge_kernel,
        grid=(t // tm,),
        in_specs=[row(D), row(DNW), row(SSM_W), row(SSM_W), row(SSM_W), row(POOL_W), row(3 * D),
                  pl.BlockSpec((1, 6, D), mod_map),
                  _const_spec((1, SSM_W)), _const_spec(glu_w.shape), _const_spec((1, SSM_W)),
                  _const_spec(pool_w.shape), _const_spec((1, POOL_W)),
                  _const_spec(wdn.shape), _const_spec(wssm.shape), _const_spec(wpool.shape),
                  _const_spec(wout.shape)],
        out_specs=row(D),
        out_shape=jax.ShapeDtypeStruct((t, D), F32),
        compiler_params=_cparams("arbitrary"),
        name="merge",
    )(x2, odn, yf, yb, u_ssm, pooled, gates, mod, ssm_d.reshape(1, SSM_W), glu_w, glu_b.reshape(1, SSM_W),
      pool_w, pool_scale.reshape(1, POOL_W), wdn, wssm, wpool, wout)


def _ffn_kernel(n_seq, final, x_ref, mod_ref, g_ref, wup_ref, cw_ref, wdn_ref, fg_ref, o_ref,
                h_s, pad0_s, pad1_s, act0_s, act1_s, acc_s):
    rows = x_ref.shape[0]
    L = rows // n_seq
    tm = 256
    pad_s = (pad0_s, pad1_s)
    act_s = (act0_s, act1_s)

    def norm_tile(it, carry):
        rs = pl.ds(pl.multiple_of(it * tm, tm), tm)
        y = _rms(x_ref[rs, :]) * g_ref[...]
        h_s[rs, :] = (y * (1.0 + mod_ref[0, 4:5, :]) + mod_ref[0, 3:4, :]).astype(BF16)
        return carry

    lax.fori_loop(0, rows // tm, norm_tile, 0)

    for slot in range(2):
        for n in range(n_seq):
            pad_s[slot][n, 0:PAD, :] = jnp.zeros((PAD, 2 * FF_BLK), F32)
            pad_s[slot][n, PAD + L:PAD + L + PAD, :] = jnp.zeros((PAD, 2 * FF_BLK), F32)
    acc_s[...] = jnp.zeros(acc_s.shape, F32)
    rc = min(L, 256)

    def gate_cols(j):
        return pl.ds(pl.multiple_of(j * FF_BLK, FF_BLK), FF_BLK)

    def val_cols(j):
        return pl.ds(pl.multiple_of(DFF + j * FF_BLK, 128), FF_BLK)

    def up(j, slot):
        h = h_s[...]
        up_g = jnp.dot(h, wup_ref[:, gate_cols(j)], preferred_element_type=F32)
        up_v = jnp.dot(h, wup_ref[:, val_cols(j)], preferred_element_type=F32)
        for n in range(n_seq):
            pad_s[slot][n, PAD:PAD + L, 0:FF_BLK] = up_g[n * L:(n + 1) * L]
            pad_s[slot][n, PAD:PAD + L, FF_BLK:2 * FF_BLK] = up_v[n * L:(n + 1) * L]

    def conv_act(j, slot):
        wg = cw_ref[:, gate_cols(j)]
        wv = cw_ref[:, val_cols(j)]
        for n in range(n_seq):
            for r0 in range(0, L, rc):
                def conv(w, cols):
                    return (w[0:1] * pad_s[slot][n, PAD - 1 + r0:PAD - 1 + r0 + rc, cols]
                            + w[1:2] * pad_s[slot][n, PAD + r0:PAD + r0 + rc, cols]
                            + w[2:3] * pad_s[slot][n, PAD + 1 + r0:PAD + 1 + r0 + rc, cols])
                hg = conv(wg, slice(0, FF_BLK))
                hv = conv(wv, slice(FF_BLK, 2 * FF_BLK))
                act_s[slot][n * L + r0:n * L + r0 + rc, :] = (_silu(hg) * hv).astype(BF16)

    def down(j, slot):
        acc_s[...] += jnp.dot(act_s[slot][...], wdn_ref[gate_cols(j), :], preferred_element_type=F32)

    n_blk = DFF // FF_BLK
    assert n_blk % 2 == 1
    up(0, 0)

    def col_pair(jj, carry):
        j = 2 * jj
        up(j + 1, 1)
        conv_act(j, 0)
        down(j, 0)
        up(j + 2, 0)
        conv_act(j + 1, 1)
        down(j + 1, 1)
        return carry

    lax.fori_loop(0, n_blk // 2, col_pair, 0)
    conv_act(n_blk - 1, 0)
    down(n_blk - 1, 0)

    def out_tile(it, carry):
        rs = pl.ds(pl.multiple_of(it * tm, tm), tm)
        y = x_ref[rs, :] + mod_ref[0, 5:6, :] * acc_s[rs, :]
        if final:
            y = _rms(y) * fg_ref[...]
        o_ref[rs, :] = y
        return carry

    lax.fori_loop(0, rows // tm, out_tile, 0)


def _ffn(x2, mod, n_tile_seq, seq_len, norm_g, wup, conv_w, wdn, final_g, final):
    t = x2.shape[0]
    rows = n_tile_seq * seq_len
    per_seq = mod.shape[0] > 1
    mod_map = (lambda i: (i, 0, 0)) if per_seq else (lambda i: (0, 0, 0))
    if per_seq:
        assert n_tile_seq == 1
    row = pl.BlockSpec((rows, D), lambda i: (i, 0))
    return pl.pallas_call(
        functools.partial(_ffn_kernel, n_tile_seq, final),
        grid=(t // rows,),
        in_specs=[row, pl.BlockSpec((1, 6, D), mod_map), _const_spec((1, D)), _const_spec(wup.shape),
                  _const_spec(conv_w.shape), _const_spec(wdn.shape), _const_spec((1, D))],
        out_specs=row,
        out_shape=jax.ShapeDtypeStruct((t, D), F32),
        scratch_shapes=[pltpu.VMEM((rows, D), BF16),
                        pltpu.VMEM((n_tile_seq, seq_len + 2 * PAD, 2 * FF_BLK), F32),
                        pltpu.VMEM((n_tile_seq, seq_len + 2 * PAD, 2 * FF_BLK), F32),
                        pltpu.VMEM((rows, FF_BLK), BF16), pltpu.VMEM((rows, FF_BLK), BF16),
                        pltpu.VMEM((rows, D), F32)],
        compiler_params=_cparams("arbitrary"),
        name="conv_ffn",
    )(x2, mod, norm_g.reshape(1, D), wup, conv_w, wdn, final_g.reshape(1, D))


def _grid_pos_embed(rows, dim):
    quarter = dim // 4
    omega = 1.0 / (10000.0 ** (jnp.arange(quarter, dtype=F32) / quarter))
    r = jnp.broadcast_to(jnp.arange(rows, dtype=F32)[:, None], (rows, GRID_W)).reshape(-1)
    col = jnp.broadcast_to(jnp.arange(GRID_W, dtype=F32)[None, :], (rows, GRID_W)).reshape(-1)

    def sincos(p):
        ang = p[:, None] * omega[None, :]
        return jnp.concatenate([jnp.sin(ang), jnp.cos(ang)], axis=-1)

    return jnp.concatenate([sincos(r), sincos(col)], axis=-1)


def _layer(x2, mod, n_seq, seq_len, n_tile_seq, s_dn, h_cols, p, final):
    qkvz, ab, u_ssm, u_pool, gates = _in_proj(x2, mod, seq_len, p["norm1_g"], p["w_in"])
    odn, st_dn = _deltanet(qkvz, ab, n_seq, seq_len, p["dn_conv"], p["dn_a_log"], p["dn_dt_bias"],
                           p["dn_norm_g"], s_dn)
    h0 = h_cols.reshape(n_seq // SUB, SUB, 2, 2 * SSM_N).transpose(0, 2, 1, 3)
    yf, yb, fin = _s5_scan(u_ssm.reshape(n_seq, seq_len, SSM_W), p["bd"], p["cm"], p["lam"], h0)
    yf = yf.reshape(n_seq * seq_len, SSM_W)
    yb = yb.reshape(n_seq * seq_len, SSM_W)
    fin = fin.transpose(0, 2, 1, 3).reshape(n_seq, 2, 2 * SSM_N)
    pooled = _pool(u_pool, n_seq, seq_len)
    x2 = _merge(x2, odn, yf, yb, u_ssm, pooled, gates, mod, seq_len, p["ssm_d"], p["glu_w"], p["ssm_glu_b"],
                p["pool_w"], p["pool_scale"], p["wdn"], p["wssm"], p["wpool"], p["wout"])
    x2 = _ffn(x2, mod, n_tile_seq, seq_len, p["norm2_g"], p["wup"], p["ffn_conv"], p["wdown"],
              p["final_norm_g"], final)
    return x2, st_dn, fin


def kernel(x_prompt, x_sample, state_dn, state_ssm_re, state_ssm_im, c, c_ctx, norm1_g, norm2_g, w_ada, b_ada, w_in, dn_conv, dn_a_log, dn_dt_bias, dn_norm_g, ssm_lambda_re, ssm_lambda_im, ssm_log_step, ssm_b_re, ssm_b_im, ssm_c_re, ssm_c_im, ssm_d, ssm_glu_w, ssm_glu_b, pool_w, pool_scale, w_branch_dn, w_branch_ssm, w_branch_pool, w_out, ffn_w_up, ffn_conv, ffn_w_down, final_norm_g):
    bc, lc, _ = x_prompt.shape
    bl, ll, _ = x_sample.shape

    n_cond = 16
    conds = jnp.zeros((n_cond, D), F32).at[0].set(c_ctx).at[1:1 + bl].set(c)
    mods = _ada(conds, w_ada, b_ada).reshape(DEPTH, n_cond, 6, D)

    x_ctx = x_prompt.reshape(bc * lc, D)
    x_lat = _add_pos(x_sample, _grid_pos_embed(ll // GRID_W, D)).reshape(bl * ll, D)

    zero_dn = jnp.zeros((bc, 2, HEADS, HD, HD), F32)
    zero_h = jnp.zeros((bc, 2, 2 * SSM_N), F32)
    new_dn, new_re, new_im = [], [], []
    for l in range(DEPTH):
        lam, bb = _s5_discretise(ssm_lambda_re[l], ssm_lambda_im[l], ssm_log_step[l], ssm_b_re[l], ssm_b_im[l])
        bd, cm = _s5_block_mats(bb, ssm_c_re[l], ssm_c_im[l])
        p = dict(
            norm1_g=norm1_g[l], norm2_g=norm2_g[l], final_norm_g=final_norm_g, w_in=_pack_w_in(w_in[l]),
            dn_conv=dn_conv[l], dn_a_log=dn_a_log[l], dn_dt_bias=dn_dt_bias[l], dn_norm_g=dn_norm_g[l],
            bd=bd, cm=cm, lam=lam, ssm_d=ssm_d[l], glu_w=ssm_glu_w[l].astype(BF16), ssm_glu_b=ssm_glu_b[l],
            pool_w=pool_w[l].astype(BF16), pool_scale=pool_scale[l],
            wdn=w_branch_dn[l].astype(BF16), wssm=w_branch_ssm[l].astype(BF16),
            wpool=w_branch_pool[l].astype(BF16), wout=w_out[l].astype(BF16),
            wup=ffn_w_up[l].astype(BF16), ffn_conv=ffn_conv[l], wdown=ffn_w_down[l].astype(BF16),
        )
        final = l == DEPTH - 1
        x_ctx, st_dn, fin = _layer(x_ctx, mods[l, 0:1], bc, lc, 4, zero_dn, zero_h, p, final)
        new_dn.append(st_dn)
        f_re, f_im = _cols_to_state(fin)
        new_re.append(f_re)
        new_im.append(f_im)
        h_lat = _state_to_cols(state_ssm_re[:, l], state_ssm_im[:, l])
        x_lat, _, _ = _layer(x_lat, mods[l, 1:1 + bl], bl, ll, 1, state_dn[:, l], h_lat, p, final)

    return (x_ctx.reshape(bc, lc, D), x_lat.reshape(bl, ll, D), jnp.stack(new_dn, axis=1),
            jnp.stack(new_re, axis=1), jnp.stack(new_im, axis=1))
```

```python
import functools

import jax
import jax.numpy as jnp
from jax import lax
from jax.experimental import pallas as pl
from jax.experimental.pallas import tpu as pltpu

F32 = jnp.float32
BF16 = jnp.bfloat16

D = 1024
DEPTH = 2
GRID_W = 64
EPS = 1e-6
HEADS = 4
HD = 128
DNW = HEADS * HD
CH = 128
N_LVL = 7
SSM_W = 512
SSM_G = 32
SSM_GC = 16
SSM_P = 64
SSM_N = SSM_G * SSM_P
SSM_BLK = 4
POOL_WINDOWS = (2, 4, 8, 16)
POOL_W = 512
POOL_G = 128
DFF = 2816
FF_BLK = 256
PAD = 8
SUB = 8
VMEM_LIMIT = 56 << 20


def _cparams(*sem):
    return pltpu.CompilerParams(dimension_semantics=sem, vmem_limit_bytes=VMEM_LIMIT)


def _const_spec(shape):
    zeros = (0,) * len(shape)
    return pl.BlockSpec(shape, lambda *_: zeros, pipeline_mode=pl.Buffered(1))


def _mm(a, b):
    return jnp.dot(a.astype(BF16), b.astype(BF16), preferred_element_type=F32)


def _mm_nt(a, b):
    return lax.dot_general(a.astype(BF16), b.astype(BF16), (((1,), (1,)), ((), ())),
                           preferred_element_type=F32)


def _mm_tn(a, b):
    return lax.dot_general(a.astype(BF16), b.astype(BF16), (((0,), (0,)), ((), ())),
                           preferred_element_type=F32)


def _mm_exact(a, b):
    return jnp.dot(a, b, precision=lax.Precision.HIGHEST, preferred_element_type=F32)


def _silu(x):
    return x * jax.nn.sigmoid(x)


def _rms(x):
    return x * lax.rsqrt(jnp.mean(x * x, axis=-1, keepdims=True) + EPS)


def _ada_kernel(c_ref, w_ref, b_ref, o_ref):
    o_ref[...] = _mm(_silu(c_ref[...]), w_ref[...]) + b_ref[...]


def _ada(conds, w_ada, b_ada):
    n = conds.shape[0]
    tn = 3072
    return pl.pallas_call(
        _ada_kernel,
        grid=(DEPTH, 6 * D // tn),
        in_specs=[pl.BlockSpec((n, D), lambda l, j: (0, 0)),
                  pl.BlockSpec((None, D, tn), lambda l, j: (l, 0, j)),
                  pl.BlockSpec((None, 1, tn), lambda l, j: (l, 0, j))],
        out_specs=pl.BlockSpec((None, n, tn), lambda l, j: (l, 0, j)),
        out_shape=jax.ShapeDtypeStruct((DEPTH, n, 6 * D), F32),
        compiler_params=_cparams("arbitrary", "arbitrary"),
        name="ada",
    )(conds, w_ada, b_ada.reshape(DEPTH, 1, 6 * D))


def _addpos_kernel(x_ref, p_ref, o_ref):
    o_ref[...] = x_ref[...] + p_ref[...]


def _add_pos(x, pos):
    b, l, _ = x.shape
    return pl.pallas_call(
        _addpos_kernel,
        grid=(b,),
        in_specs=[pl.BlockSpec((None, l, D), lambda i: (i, 0, 0)), _const_spec((l, D))],
        out_specs=pl.BlockSpec((None, l, D), lambda i: (i, 0, 0)),
        out_shape=jax.ShapeDtypeStruct(x.shape, F32),
        compiler_params=_cparams("arbitrary"),
        name="add_pos",
    )(x, pos)


IN_AB = 4 * DNW
IN_SSM = IN_AB + 128
IN_POOL = IN_SSM + SSM_W
IN_GATE = IN_POOL + POOL_W
IN_END = IN_GATE + 3 * D


def _inproj_kernel(x_ref, mod_ref, g_ref, w_ref, qkvz_ref, ab_ref, ussm_ref, upool_ref, gate_ref):
    y = _rms(x_ref[...]) * g_ref[...]
    h = (y * (1.0 + mod_ref[0, 1:2, :]) + mod_ref[0, 0:1, :]).astype(BF16)
    qkvz_ref[...] = jnp.dot(h, w_ref[:, 0:IN_AB], preferred_element_type=F32)
    ab_ref[...] = jnp.dot(h, w_ref[:, IN_AB:IN_SSM], preferred_element_type=F32)
    ussm_ref[...] = jnp.dot(h, w_ref[:, IN_SSM:IN_POOL], preferred_element_type=F32)
    upool_ref[...] = jnp.dot(h, w_ref[:, IN_POOL:IN_GATE], preferred_element_type=F32)
    gate_ref[...] = jnp.dot(h, w_ref[:, IN_GATE:IN_END], preferred_element_type=F32).astype(BF16)


TOKEN_TILE = 512


def _pack_w_in(w_in):
    n_ab = 4 * HEADS
    return jnp.concatenate([w_in[:, :IN_AB], w_in[:, IN_AB:IN_AB + n_ab], jnp.zeros((D, 128 - n_ab), F32),
                            w_in[:, IN_AB + n_ab:]], axis=1).astype(BF16)


def _mod_map(mod, seq_len, tm):
    if mod.shape[0] == 1:
        return lambda i: (0, 0, 0)
    assert seq_len % tm == 0
    return lambda i: (i // (seq_len // tm), 0, 0)


def _in_proj(x2, mod, seq_len, norm_g, w):
    t = x2.shape[0]
    tm = TOKEN_TILE
    mod_map = _mod_map(mod, seq_len, tm)
    row = lambda w: pl.BlockSpec((tm, w), lambda i: (i, 0))
    return pl.pallas_call(
        _inproj_kernel,
        grid=(t // tm,),
        in_specs=[row(D), pl.BlockSpec((1, 6, D), mod_map), _const_spec((1, D)), _const_spec(w.shape)],
        out_specs=[row(4 * DNW), row(128), row(SSM_W), row(POOL_W), row(3 * D)],
        out_shape=[jax.ShapeDtypeStruct((t, 4 * DNW), F32), jax.ShapeDtypeStruct((t, 128), F32),
                   jax.ShapeDtypeStruct((t, SSM_W), F32),
                   jax.ShapeDtypeStruct((t, POOL_W), F32),
                   jax.ShapeDtypeStruct((t, 3 * D), BF16)],
        compiler_params=_cparams("arbitrary"),
        name="in_proj",
    )(x2, mod, norm_g.reshape(1, D), w)


def _dn_kernel(qkv_ref, z_ref, ab_ref, cw_ref, alog_ref, dtb_ref, ng_ref, s0_ref,
               o_ref, sfin_ref,
               pad_s, q_s, k_s, v_s, gc_s, gr_s, b_s, o_s, st_s, lm_s):
    L = o_ref.shape[0]
    nc = L // CH

    ab = ab_ref[...]
    xs = ab + dtb_ref[...]
    softplus = jnp.maximum(xs, 0.0) + jnp.log1p(jnp.exp(-jnp.abs(xs)))
    g = -jnp.exp(alog_ref[...]) * softplus
    gt = g.T
    b_s[...] = jax.nn.sigmoid(ab)
    ri = lax.broadcasted_iota(jnp.int32, (CH, CH), 0)
    ci = lax.broadcasted_iota(jnp.int32, (CH, CH), 1)
    lo = (ri >= ci).astype(F32)
    up = (ri <= ci).astype(F32)
    for c in range(nc):
        r = slice(c * CH, (c + 1) * CH)
        for d, (tri, tri_t) in enumerate(((lo, up), (up, lo))):
            gc_s[d, r, :] = _mm_exact(tri, g[r, :])
            gr_s[d, :, r] = _mm_exact(gt[:, r], tri_t)

    pad_s[0:PAD, :] = jnp.zeros((PAD, 3 * DNW), F32)
    pad_s[PAD + L:PAD + L + PAD, :] = jnp.zeros((PAD, 3 * DNW), F32)
    pad_s[PAD:PAD + L, :] = qkv_ref[...]
    rc = 256
    for cb in range(3 * HEADS):
        cols = slice(cb * HD, (cb + 1) * HD)
        w = cw_ref[:, cols]
        for r0 in range(0, L, rc):
            y = (w[0:1] * pad_s[PAD - 1 + r0:PAD - 1 + r0 + rc, cols]
                 + w[1:2] * pad_s[PAD + r0:PAD + r0 + rc, cols]
                 + w[2:3] * pad_s[PAD + 1 + r0:PAD + 1 + r0 + rc, cols])
            y = _silu(y)
            h = cb % HEADS
            hc = slice(h * HD, (h + 1) * HD)
            if cb < HEADS:
                y = y * lax.rsqrt(jnp.sum(y * y, axis=-1, keepdims=True) + EPS) * (HD ** -0.5)
                q_s[r0:r0 + rc, hc] = y
            elif cb < 2 * HEADS:
                y = y * lax.rsqrt(jnp.sum(y * y, axis=-1, keepdims=True) + EPS)
                k_s[r0:r0 + rc, hc] = y
            else:
                v_s[r0:r0 + rc, hc] = y

    for d in range(2):
        for h in range(HEADS):
            st_s[d * HEADS + h] = s0_ref[d, h]
    o_s[...] = jnp.zeros(o_s.shape, F32)

    eye = (ri == ci).astype(F32)
    for lvl in range(N_LVL):
        pair = ((ri >> (lvl + 1)) == (ci >> (lvl + 1))) & ((ri >> lvl) != (ci >> lvl))
        lm_s[lvl] = pair.astype(F32)

    def chunk_steps(cs):
        incl = [ri >= ci, ri <= ci]
        strict = [ri > ci, ri < ci]
        last = [CH - 1, 0]
        hcs = [slice(h * HD, (h + 1) * HD) for h in range(HEADS)]
        rows, gcol, grow = {}, {}, {}
        for n, c in enumerate(cs):
            for d in range(2):
                cidx = c if d == 0 else nc - 1 - c
                r = pl.ds(pl.multiple_of(cidx * CH, CH), CH)
                rows[n, d] = r
                gcol[n, d] = gc_s[d, r, :]
                grow[n, d] = gr_s[d, :, r]
        keys = [(n, d, h) for n in range(len(cs)) for d in range(2) for h in range(HEADS)]
        q, k, v, gc, gtot, beta, decay, eg, kb = ({} for _ in range(9))
        for key in keys:
            n, d, h = key
            s = d * HEADS + h
            r = rows[n, d]
            q[key] = q_s[r, hcs[h]]
            k[key] = k_s[r, hcs[h]]
            v[key] = v_s[r, hcs[h]]
            gc[key] = gcol[n, d][:, s:s + 1]
            gtot[key] = gcol[n, d][last[d]:last[d] + 1, s:s + 1]
            beta[key] = b_s[r, 2 * HEADS + s:2 * HEADS + s + 1]
            decay[key] = jnp.where(incl[d], jnp.exp(gc[key] - grow[n, d][s:s + 1, :]), 0.0)
            eg[key] = jnp.exp(gc[key])
            kb[key] = k[key] * beta[key]
        raw = {key: _mm_nt(jnp.concatenate([kb[key], q[key]], axis=0), k[key]) for key in keys}
        a = {key: jnp.where(strict[key[1]], raw[key][:CH] * decay[key], 0.0) for key in keys}
        qk = {key: raw[key][CH:] * decay[key] for key in keys}
        t = {key: eye - a[key] * lm_s[0] for key in keys}
        for lvl in range(1, N_LVL):
            x = {key: _mm(a[key] * lm_s[lvl], t[key]) for key in keys}
            t = {key: t[key] - _mm(t[key], x[key]) for key in keys}
        uw = {key: _mm(t[key], jnp.concatenate([v[key] * beta[key], kb[key] * eg[key]], axis=1)) for key in keys}
        for n in range(len(cs)):
            step = [key for key in keys if key[0] == n]
            st = {key: st_s[key[1] * HEADS + key[2]] for key in step}
            ws = {key: _mm(jnp.concatenate([uw[key][:, HD:], q[key] * eg[key]], axis=0), st[key]) for key in step}
            v_new = {key: uw[key][:, :HD] - ws[key][:CH] for key in step}
            o = {key: ws[key][CH:] + _mm(qk[key], v_new[key]) for key in step}
            for key in step:
                st_s[key[1] * HEADS + key[2]] = (st[key] * jnp.exp(gtot[key])
                                                 + _mm_tn(k[key] * jnp.exp(gtot[key] - gc[key]), v_new[key]))
            for key in step:
                r, hc = rows[n, key[1]], hcs[key[2]]
                o_s[r, hc] = o_s[r, hc] + o[key]

    steps_per_iter = 2
    assert nc % steps_per_iter == 0

    def chunk_iter(it, carry):
        chunk_steps([it * steps_per_iter + n for n in range(steps_per_iter)])
        return carry

    lax.fori_loop(0, nc // steps_per_iter, chunk_iter, 0)

    for d in range(2):
        for h in range(HEADS):
            sfin_ref[d, h] = st_s[d * HEADS + h]
    for h in range(HEADS):
        hc = slice(h * HD, (h + 1) * HD)
        for r0 in range(0, L, rc):
            z = z_ref[r0:r0 + rc, hc]
            o_ref[r0:r0 + rc, hc] = (_rms(o_s[r0:r0 + rc, hc]) * ng_ref[...] * _silu(z)).astype(BF16)


def _deltanet(qkvz, ab, n_seq, seq_len, conv_w, a_log, dt_bias, norm_g, s0):
    L = seq_len
    pad_row = lambda v: jnp.zeros((1, 128), F32).at[0, :2 * HEADS].set(v.reshape(-1))
    st_spec = pl.BlockSpec((None, 2, HEADS, HD, HD), lambda i: (i, 0, 0, 0, 0))
    return pl.pallas_call(
        _dn_kernel,
        grid=(n_seq,),
        in_specs=[pl.BlockSpec((L, 3 * DNW), lambda i: (i, 0)),
                  pl.BlockSpec((L, DNW), lambda i: (i, 3)),
                  pl.BlockSpec((L, 128), lambda i: (i, 0)),
                  _const_spec((3, 3 * DNW)), _const_spec((1, 128)), _const_spec((1, 128)),
                  _const_spec((1, HD)), st_spec],
        out_specs=[pl.BlockSpec((L, DNW), lambda i: (i, 0)), st_spec],
        out_shape=[jax.ShapeDtypeStruct((n_seq * L, DNW), BF16),
                   jax.ShapeDtypeStruct((n_seq, 2, HEADS, HD, HD), F32)],
        scratch_shapes=[pltpu.VMEM((L + 2 * PAD, 3 * DNW), F32),
                        pltpu.VMEM((L, DNW), F32), pltpu.VMEM((L, DNW), F32), pltpu.VMEM((L, DNW), F32),
                        pltpu.VMEM((2, L, 128), F32), pltpu.VMEM((2, 128, L), F32), pltpu.VMEM((L, 128), F32),
                        pltpu.VMEM((L, DNW), F32), pltpu.VMEM((2 * HEADS, HD, HD), F32),
                        pltpu.VMEM((N_LVL, CH, CH), F32)],
        compiler_params=_cparams("arbitrary"),
        name="deltanet",
    )(qkvz, qkvz, ab, conv_w, pad_row(a_log), pad_row(dt_bias), norm_g.reshape(1, HD), s0)


def _s5disc_kernel(lre_ref, lim_ref, ls_ref, bre_ref, bim_ref, lam_ref, bb_ref):
    lam_re = lre_ref[...]
    lam_im = lim_ref[...]
    step = jnp.exp(ls_ref[...])
    mag = jnp.exp(lam_re * step)
    ang = lam_im * step
    lb_re = mag * jnp.cos(ang)
    lb_im = mag * jnp.sin(ang)
    nr, ni = lb_re - 1.0, lb_im
    den = lam_re * lam_re + lam_im * lam_im
    f_re = (nr * lam_re + ni * lam_im) / den
    f_im = (ni * lam_re - nr * lam_im) / den
    lam_ref[:, 0, :] = lb_re
    lam_ref[:, 1, :] = lb_im
    b_re = bre_ref[...]
    b_im = bim_ref[...]
    for d in range(2):
        fr = f_re[d:d + 1]
        fi = f_im[d:d + 1]
        bb_ref[d, 0] = fr * b_re - fi * b_im
        bb_ref[d, 1] = fr * b_im + fi * b_re


def _s5_discretise(lam_re, lam_im, log_step, b_re, b_im):
    ls = jnp.repeat(log_step, SSM_P, axis=1)
    bt = lambda b: b.reshape(SSM_N, SSM_GC).T
    return pl.pallas_call(
        _s5disc_kernel,
        out_shape=[jax.ShapeDtypeStruct((2, 2, SSM_N), F32),
                   jax.ShapeDtypeStruct((2, 2, SSM_GC, SSM_N), F32)],
        name="s5_discretise",
    )(lam_re.reshape(2, SSM_N), lam_im.reshape(2, SSM_N), ls, bt(b_re), bt(b_im))


def _s5_block_mats(bb, c_re, c_im):
    eye = jnp.eye(SUB, dtype=F32)
    b6 = bb.reshape(2, 2, SSM_GC, SSM_BLK, SUB, SSM_P)
    bd = jnp.einsum("drcjgp,gh->djgcrhp", b6, eye).reshape(2, SSM_BLK, SUB * SSM_GC, 2 * SUB * SSM_P)
    c2 = jnp.stack([c_re, -c_im], axis=0).reshape(2, SSM_BLK, SUB, SSM_GC, SSM_P)
    cm = jnp.einsum("rjgcp,gh->jrgphc", c2, eye).reshape(SSM_BLK, 2 * SUB * SSM_P, SUB * SSM_GC)
    return bd.astype(BF16), cm.astype(BF16)


def _s5_kernel(uf_ref, ub_ref, bd_ref, cm_ref, lam_ref, h0_ref, yf_ref, yb_ref, fin_ref,
               xf_s, xb_s, uf_s, ub_s, yf_s, yb_s, car_s):
    i = pl.program_id(1)
    tl = uf_ref.shape[1]
    rows = tl * SUB
    bw = 2 * SUB * SSM_P
    half = SUB * SSM_P

    @pl.when(i == 0)
    def _():
        car_s[...] = h0_ref[...]

    def seq_rows(s):
        return pl.ds(s, tl, stride=SUB)

    blk_w = SUB * SSM_GC

    def expand(d, u_ref, ut_s, x_s):
        for s in range(SUB):
            for j in range(SSM_BLK):
                ut_s[j, seq_rows(s), :] = u_ref[s, :, j * blk_w:(j + 1) * blk_w]
        for j in range(SSM_BLK):
            x_s[:, j * bw:(j + 1) * bw] = jnp.dot(ut_s[j].astype(BF16), bd_ref[d, j],
                                                  preferred_element_type=F32)

    def scan(d, x_s):
        lw = 256
        for j in range(SSM_BLK):
            for kk in range(half // lw):
                cre = slice(j * bw + kk * lw, j * bw + (kk + 1) * lw)
                cim = slice(j * bw + half + kk * lw, j * bw + half + (kk + 1) * lw)
                n0 = j * half + kk * lw
                lr = jnp.broadcast_to(lam_ref[d, 0:1, n0:n0 + lw], (SUB, lw))
                li = jnp.broadcast_to(lam_ref[d, 1:2, n0:n0 + lw], (SUB, lw))
                xr = car_s[d, :, cre]
                xi = car_s[d, :, cim]
                for t in (range(tl) if d == 0 else range(tl - 1, -1, -1)):
                    r = slice(t * SUB, (t + 1) * SUB)
                    xr, xi = lr * xr - li * xi + x_s[r, cre], lr * xi + li * xr + x_s[r, cim]
                    x_s[r, cre] = xr
                    x_s[r, cim] = xi
                car_s[d, :, cre] = xr
                car_s[d, :, cim] = xi

    def project(x_s, yt_s, y_ref):
        for j in range(SSM_BLK):
            yt_s[j] = jnp.dot(x_s[:, j * bw:(j + 1) * bw].astype(BF16), cm_ref[j], preferred_element_type=F32)
        for s in range(SUB):
            for j in range(SSM_BLK):
                y_ref[s, :, j * blk_w:(j + 1) * blk_w] = yt_s[j, seq_rows(s), :]

    expand(0, uf_ref, uf_s, xf_s)
    scan(0, xf_s)
    expand(1, ub_ref, ub_s, xb_s)
    project(xf_s, yf_s, yf_ref)
    scan(1, xb_s)
    project(xb_s, yb_s, yb_ref)

    @pl.when(i == pl.num_programs(1) - 1)
    def _():
        fin_ref[...] = car_s[...]


def _s5_scan(u, bd, cm, lam, h0):
    n_seq, L = u.shape[0], u.shape[1]
    n_grp = n_seq // SUB
    tl = 32
    nt = L // tl
    ncol = 2 * SSM_N
    fwd = pl.BlockSpec((SUB, tl, SSM_W), lambda g, i: (g, i, 0))
    bwd = pl.BlockSpec((SUB, tl, SSM_W), lambda g, i: (g, nt - 1 - i, 0))
    st = pl.BlockSpec((None, 2, SUB, ncol), lambda g, i: (g, 0, 0, 0))
    tile = lambda w: pltpu.VMEM((tl * SUB, w), F32)
    relayout = pltpu.VMEM((SSM_BLK, tl * SUB, SUB * SSM_GC), F32)
    return pl.pallas_call(
        _s5_kernel,
        grid=(n_grp, nt),
        in_specs=[fwd, bwd, _const_spec(bd.shape), _const_spec(cm.shape), _const_spec(lam.shape), st],
        out_specs=[fwd, bwd, st],
        out_shape=[jax.ShapeDtypeStruct(u.shape, F32), jax.ShapeDtypeStruct(u.shape, F32),
                   jax.ShapeDtypeStruct((n_grp, 2, SUB, ncol), F32)],
        scratch_shapes=[tile(ncol), tile(ncol), relayout, relayout, relayout, relayout,
                        pltpu.VMEM((2, SUB, ncol), F32)],
        compiler_params=_cparams("arbitrary", "arbitrary"),
        name="s5_scan",
    )(u, u, bd, cm, lam, h0)


def _state_to_cols(h_re, h_im):
    b = h_re.shape[0]
    r = h_re.reshape(b, 2, SSM_BLK, SUB * SSM_P)
    m = h_im.reshape(b, 2, SSM_BLK, SUB * SSM_P)
    return jnp.stack([r, m], axis=3).reshape(b, 2, 2 * SSM_N)


def _cols_to_state(cols):
    b = cols.shape[0]
    c = cols.reshape(b, 2, SSM_BLK, 2, SUB * SSM_P)
    return (c[:, :, :, 0].reshape(b, 2, SSM_G, SSM_P), c[:, :, :, 1].reshape(b, 2, SSM_G, SSM_P))


def _merge_kernel(seq_len, x_ref, odn_ref, yf_ref, yb_ref, ussm_ref, up_ref, up_prev_ref, up_next_ref,
                  gate_ref, mod_ref, sd_ref, gw_ref, gb_ref, pw_ref, ps_ref, wdn_ref, wssm_ref, wpool_ref,
                  wout_ref, o_ref, pad_s, pool_s):
    tm = x_ref.shape[0]
    L = seq_len

    tile_pos = lax.rem(pl.program_id(0) * tm, L)
    interior = tm > L
    if interior:
        pad_s[0:PAD, :] = jnp.zeros((PAD, POOL_W), F32)
        pad_s[PAD + tm:PAD + tm + PAD, :] = jnp.zeros((PAD, POOL_W), F32)
    else:
        pad_s[0:PAD, :] = jnp.where(tile_pos == 0, 0.0, up_prev_ref[...])
        pad_s[PAD + tm:PAD + tm + PAD, :] = jnp.where(tile_pos + tm == L, 0.0, up_next_ref[...])
    pad_s[PAD:PAD + tm, :] = up_ref[...]
    rc = 256
    for gi, w in enumerate(POOL_WINDOWS):
        cols = slice(gi * POOL_G, (gi + 1) * POOL_G)
        for r0 in range(0, tm, rc):
            p = lax.rem(tile_pos + r0 + lax.broadcasted_iota(jnp.int32, (rc, POOL_G), 0), L)
            acc = jnp.zeros((rc, POOL_G), F32)
            for dlt in range(-(w // 2), w // 2):
                term = pad_s[PAD + r0 + dlt:PAD + r0 + dlt + rc, cols]
                if interior:
                    term = jnp.where((p + dlt >= 0) if dlt < 0 else (p + dlt < L), term, 0.0)
                acc = acc + term
            cnt = jnp.minimum(p + w // 2, L) - jnp.maximum(p - w // 2, 0)
            pooled = acc / cnt.astype(F32) - pad_s[PAD + r0:PAD + r0 + rc, cols]
            pool_s[r0:r0 + rc, cols] = pooled.astype(BF16)

    y = yf_ref[...] + yb_ref[...] + sd_ref[...] * ussm_ref[...]
    y = jax.nn.gelu(y)
    o_ssm = y * jax.nn.sigmoid(_mm(y, gw_ref[...]) + gb_ref[...])
    o_pool = jnp.concatenate(
        [_mm(pool_s[:, gi * POOL_G:(gi + 1) * POOL_G], pw_ref[gi]) for gi in range(len(POOL_WINDOWS))],
        axis=1) * ps_ref[...]
    gate = lambda i: jax.nn.sigmoid(gate_ref[:, i * D:(i + 1) * D].astype(F32))
    merged = (gate(0) * _mm(odn_ref[...], wdn_ref[...]) + gate(1) * _mm(o_ssm, wssm_ref[...])
              + gate(2) * _mm(o_pool, wpool_ref[...]))
    o_ref[...] = x_ref[...] + mod_ref[0, 2:3, :] * _mm(merged, wout_ref[...])


def _merge(x2, odn, yf, yb, u_ssm, u_pool, gates, mod, seq_len, ssm_d, glu_w, glu_b, pool_w, pool_scale,
           wdn, wssm, wpool, wout):
    t = x2.shape[0]
    tm = TOKEN_TILE
    assert tm % seq_len == 0 or seq_len % tm == 0
    mod_map = _mod_map(mod, seq_len, tm)
    row = lambda w: pl.BlockSpec((tm, w), lambda i: (i, 0))
    per_tile = tm // PAD
    prev = pl.BlockSpec((PAD, POOL_W), lambda i: (jnp.maximum(i * per_tile - 1, 0), 0))
    nxt = pl.BlockSpec((PAD, POOL_W), lambda i: (jnp.minimum((i + 1) * per_tile, t // PAD - 1), 0))
    return pl.pallas_call(
        functools.partial(_merge_kernel, seq_len),
        grid=(t // tm,),
        in_specs=[row(D), row(DNW), row(SSM_W), row(SSM_W), row(SSM_W), row(POOL_W), prev, nxt, row(3 * D),
                  pl.BlockSpec((1, 6, D), mod_map),
                  _const_spec((1, SSM_W)), _const_spec(glu_w.shape), _const_spec((1, SSM_W)),
                  _const_spec(pool_w.shape), _const_spec((1, POOL_W)),
                  _const_spec(wdn.shape), _const_spec(wssm.shape), _const_spec(wpool.shape),
                  _const_spec(wout.shape)],
        out_specs=row(D),
        out_shape=jax.ShapeDtypeStruct((t, D), F32),
        scratch_shapes=[pltpu.VMEM((tm + 2 * PAD, POOL_W), F32), pltpu.VMEM((tm, POOL_W), BF16)],
        compiler_params=_cparams("arbitrary"),
        name="merge",
    )(x2, odn, yf, yb, u_ssm, u_pool, u_pool, u_pool, gates, mod, ssm_d.reshape(1, SSM_W), glu_w,
      glu_b.reshape(1, SSM_W), pool_w, pool_scale.reshape(1, POOL_W), wdn, wssm, wpool, wout)


def _ffn_kernel(n_seq, final, x_ref, mod_ref, g_ref, wup_ref, cw_ref, wdn_ref, fg_ref, o_ref,
                h_s, pad0_s, pad1_s, pad2_s, act0_s, act1_s, acc_s):
    rows = x_ref.shape[0]
    L = rows // n_seq
    tm = 256
    pad_s = (pad0_s, pad1_s, pad2_s)
    act_s = (act0_s, act1_s)

    def norm_tile(it, carry):
        rs = pl.ds(pl.multiple_of(it * tm, tm), tm)
        y = _rms(x_ref[rs, :]) * g_ref[...]
        h_s[rs, :] = (y * (1.0 + mod_ref[0, 4:5, :]) + mod_ref[0, 3:4, :]).astype(BF16)
        return carry

    lax.fori_loop(0, rows // tm, norm_tile, 0)

    for slot in range(len(pad_s)):
        for n in range(n_seq):
            pad_s[slot][n, 0:PAD, :] = jnp.zeros((PAD, 2 * FF_BLK), F32)
            pad_s[slot][n, PAD + L:PAD + L + PAD, :] = jnp.zeros((PAD, 2 * FF_BLK), F32)
    acc_s[...] = jnp.zeros(acc_s.shape, F32)
    rc = min(L, 256)

    def gate_cols(j):
        return slice(j * FF_BLK, (j + 1) * FF_BLK)

    def val_cols(j):
        return slice(DFF + j * FF_BLK, DFF + (j + 1) * FF_BLK)

    ur = min(L, 512)

    def up(j, slot):
        for n in range(n_seq):
            for r0 in range(0, L, ur):
                h = h_s[n * L + r0:n * L + r0 + ur, :]
                pad_s[slot][n, PAD + r0:PAD + r0 + ur, 0:FF_BLK] = jnp.dot(
                    h, wup_ref[:, gate_cols(j)], preferred_element_type=F32)
                pad_s[slot][n, PAD + r0:PAD + r0 + ur, FF_BLK:2 * FF_BLK] = jnp.dot(
                    h, wup_ref[:, val_cols(j)], preferred_element_type=F32)

    def conv_act(j, slot, act_slot):
        wg = cw_ref[:, gate_cols(j)]
        wv = cw_ref[:, val_cols(j)]
        for n in range(n_seq):
            for r0 in range(0, L, rc):
                def conv(w, cols):
                    return (w[0:1] * pad_s[slot][n, PAD - 1 + r0:PAD - 1 + r0 + rc, cols]
                            + w[1:2] * pad_s[slot][n, PAD + r0:PAD + r0 + rc, cols]
                            + w[2:3] * pad_s[slot][n, PAD + 1 + r0:PAD + 1 + r0 + rc, cols])
                hg = conv(wg, slice(0, FF_BLK))
                hv = conv(wv, slice(FF_BLK, 2 * FF_BLK))
                act_s[act_slot][n * L + r0:n * L + r0 + rc, :] = (_silu(hg) * hv).astype(BF16)

    def down(j, slot):
        acc_s[...] += jnp.dot(act_s[slot][...], wdn_ref[gate_cols(j), :], preferred_element_type=F32)

    n_blk = DFF // FF_BLK
    look = len(pad_s)
    for j in range(min(look, n_blk)):
        up(j, j % look)
    for j in range(n_blk):
        conv_act(j, j % look, j % 2)
        down(j, j % 2)
        if j + look < n_blk:
            up(j + look, j % look)

    def out_tile(it, carry):
        rs = pl.ds(pl.multiple_of(it * tm, tm), tm)
        y = x_ref[rs, :] + mod_ref[0, 5:6, :] * acc_s[rs, :]
        if final:
            y = _rms(y) * fg_ref[...]
        o_ref[rs, :] = y
        return carry

    lax.fori_loop(0, rows // tm, out_tile, 0)


def _ffn(x2, mod, n_tile_seq, seq_len, norm_g, wup, conv_w, wdn, final_g, final):
    t = x2.shape[0]
    rows = n_tile_seq * seq_len
    per_seq = mod.shape[0] > 1
    mod_map = (lambda i: (i, 0, 0)) if per_seq else (lambda i: (0, 0, 0))
    if per_seq:
        assert n_tile_seq == 1
    row = pl.BlockSpec((rows, D), lambda i: (i, 0))
    return pl.pallas_call(
        functools.partial(_ffn_kernel, n_tile_seq, final),
        grid=(t // rows,),
        in_specs=[row, pl.BlockSpec((1, 6, D), mod_map), _const_spec((1, D)), _const_spec(wup.shape),
                  _const_spec(conv_w.shape), _const_spec(wdn.shape), _const_spec((1, D))],
        out_specs=row,
        out_shape=jax.ShapeDtypeStruct((t, D), F32),
        scratch_shapes=[pltpu.VMEM((rows, D), BF16),
                        pltpu.VMEM((n_tile_seq, seq_len + 2 * PAD, 2 * FF_BLK), F32),
                        pltpu.VMEM((n_tile_seq, seq_len + 2 * PAD, 2 * FF_BLK), F32),
                        pltpu.VMEM((n_tile_seq, seq_len + 2 * PAD, 2 * FF_BLK), F32),
                        pltpu.VMEM((rows, FF_BLK), BF16), pltpu.VMEM((rows, FF_BLK), BF16),
                        pltpu.VMEM((rows, D), F32)],
        compiler_params=_cparams("arbitrary"),
        name="conv_ffn",
    )(x2, mod, norm_g.reshape(1, D), wup, conv_w, wdn, final_g.reshape(1, D))


def _grid_pos_embed(rows, dim):
    quarter = dim // 4
    omega = 1.0 / (10000.0 ** (jnp.arange(quarter, dtype=F32) / quarter))
    r = jnp.broadcast_to(jnp.arange(rows, dtype=F32)[:, None], (rows, GRID_W)).reshape(-1)
    col = jnp.broadcast_to(jnp.arange(GRID_W, dtype=F32)[None, :], (rows, GRID_W)).reshape(-1)

    def sincos(p):
        ang = p[:, None] * omega[None, :]
        return jnp.concatenate([jnp.sin(ang), jnp.cos(ang)], axis=-1)

    return jnp.concatenate([sincos(r), sincos(col)], axis=-1)


def _layer(x2, mod, n_seq, seq_len, n_tile_seq, s_dn, h_cols, p, final):
    qkvz, ab, u_ssm, u_pool, gates = _in_proj(x2, mod, seq_len, p["norm1_g"], p["w_in"])
    odn, st_dn = _deltanet(qkvz, ab, n_seq, seq_len, p["dn_conv"], p["dn_a_log"], p["dn_dt_bias"],
                           p["dn_norm_g"], s_dn)
    h0 = h_cols.reshape(n_seq // SUB, SUB, 2, 2 * SSM_N).transpose(0, 2, 1, 3)
    yf, yb, fin = _s5_scan(u_ssm.reshape(n_seq, seq_len, SSM_W), p["bd"], p["cm"], p["lam"], h0)
    yf = yf.reshape(n_seq * seq_len, SSM_W)
    yb = yb.reshape(n_seq * seq_len, SSM_W)
    fin = fin.transpose(0, 2, 1, 3).reshape(n_seq, 2, 2 * SSM_N)
    x2 = _merge(x2, odn, yf, yb, u_ssm, u_pool, gates, mod, seq_len, p["ssm_d"], p["glu_w"], p["ssm_glu_b"],
                p["pool_w"], p["pool_scale"], p["wdn"], p["wssm"], p["wpool"], p["wout"])
    x2 = _ffn(x2, mod, n_tile_seq, seq_len, p["norm2_g"], p["wup"], p["ffn_conv"], p["wdown"],
              p["final_norm_g"], final)
    return x2, st_dn, fin


def kernel(x_prompt, x_sample, state_dn, state_ssm_re, state_ssm_im, c, c_ctx, norm1_g, norm2_g, w_ada, b_ada, w_in, dn_conv, dn_a_log, dn_dt_bias, dn_norm_g, ssm_lambda_re, ssm_lambda_im, ssm_log_step, ssm_b_re, ssm_b_im, ssm_c_re, ssm_c_im, ssm_d, ssm_glu_w, ssm_glu_b, pool_w, pool_scale, w_branch_dn, w_branch_ssm, w_branch_pool, w_out, ffn_w_up, ffn_conv, ffn_w_down, final_norm_g):
    bc, lc, _ = x_prompt.shape
    bl, ll, _ = x_sample.shape

    n_cond = 16
    conds = jnp.zeros((n_cond, D), F32).at[0].set(c_ctx).at[1:1 + bl].set(c)
    mods = _ada(conds, w_ada, b_ada).reshape(DEPTH, n_cond, 6, D)

    x_ctx = x_prompt.reshape(bc * lc, D)
    x_lat = _add_pos(x_sample, _grid_pos_embed(ll // GRID_W, D)).reshape(bl * ll, D)

    zero_dn = jnp.zeros((bc, 2, HEADS, HD, HD), F32)
    zero_h = jnp.zeros((bc, 2, 2 * SSM_N), F32)
    new_dn, new_re, new_im = [], [], []
    for l in range(DEPTH):
        lam, bb = _s5_discretise(ssm_lambda_re[l], ssm_lambda_im[l], ssm_log_step[l], ssm_b_re[l], ssm_b_im[l])
        bd, cm = _s5_block_mats(bb, ssm_c_re[l], ssm_c_im[l])
        p = dict(
            norm1_g=norm1_g[l], norm2_g=norm2_g[l], final_norm_g=final_norm_g, w_in=_pack_w_in(w_in[l]),
            dn_conv=dn_conv[l], dn_a_log=dn_a_log[l], dn_dt_bias=dn_dt_bias[l], dn_norm_g=dn_norm_g[l],
            bd=bd, cm=cm, lam=lam, ssm_d=ssm_d[l], glu_w=ssm_glu_w[l].astype(BF16), ssm_glu_b=ssm_glu_b[l],
            pool_w=pool_w[l].astype(BF16), pool_scale=pool_scale[l],
            wdn=w_branch_dn[l].astype(BF16), wssm=w_branch_ssm[l].astype(BF16),
            wpool=w_branch_pool[l].astype(BF16), wout=w_out[l].astype(BF16),
            wup=ffn_w_up[l].astype(BF16), ffn_conv=ffn_conv[l], wdown=ffn_w_down[l].astype(BF16),
        )
        final = l == DEPTH - 1
        x_ctx, st_dn, fin = _layer(x_ctx, mods[l, 0:1], bc, lc, 4, zero_dn, zero_h, p, final)
        new_dn.append(st_dn)
        f_re, f_im = _cols_to_state(fin)
        new_re.append(f_re)
        new_im.append(f_im)
        h_lat = _state_to_cols(state_ssm_re[:, l], state_ssm_im[:, l])
        x_lat, _, _ = _layer(x_lat, mods[l, 1:1 + bl], bl, ll, 1, state_dn[:, l], h_lat, p, final)

    return (x_ctx.reshape(bc, lc, D), x_lat.reshape(bl, ll, D), jnp.stack(new_dn, axis=1),
            jnp.stack(new_re, axis=1), jnp.stack(new_im, axis=1))
```

```python
import functools

import jax
import jax.numpy as jnp
from jax import lax
from jax.experimental import pallas as pl
from jax.experimental.pallas import tpu as pltpu

F32 = jnp.float32
BF16 = jnp.bfloat16

D = 1024
DEPTH = 2
GRID_W = 64
EPS = 1e-6
HEADS = 4
HD = 128
DNW = HEADS * HD
CH = 128
N_LVL = 7
SSM_W = 512
SSM_G = 32
SSM_GC = 16
SSM_P = 64
SSM_N = SSM_G * SSM_P
SSM_BLK = 4
POOL_WINDOWS = (2, 4, 8, 16)
POOL_W = 512
POOL_G = 128
DFF = 2816
FF_BLK = 256
PAD = 8
SUB = 8
VMEM_LIMIT = 56 << 20


def _cparams(*sem):
    return pltpu.CompilerParams(dimension_semantics=sem, vmem_limit_bytes=VMEM_LIMIT)


def _const_spec(shape):
    zeros = (0,) * len(shape)
    return pl.BlockSpec(shape, lambda *_: zeros, pipeline_mode=pl.Buffered(1))


def _mm(a, b):
    return jnp.dot(a.astype(BF16), b.astype(BF16), preferred_element_type=F32)


def _mm_nt(a, b):
    return lax.dot_general(a.astype(BF16), b.astype(BF16), (((1,), (1,)), ((), ())),
                           preferred_element_type=F32)


def _mm_tn(a, b):
    return lax.dot_general(a.astype(BF16), b.astype(BF16), (((0,), (0,)), ((), ())),
                           preferred_element_type=F32)


def _mm_exact(a, b):
    return jnp.dot(a, b, precision=lax.Precision.HIGHEST, preferred_element_type=F32)


def _silu(x):
    return x * jax.nn.sigmoid(x)


def _rms(x):
    return x * lax.rsqrt(jnp.mean(x * x, axis=-1, keepdims=True) + EPS)


def _ada_kernel(c_ref, w_ref, b_ref, o_ref):
    o_ref[...] = _mm(_silu(c_ref[...]), w_ref[...]) + b_ref[...]


def _ada(conds, w_ada, b_ada):
    n = conds.shape[0]
    tn = 3072
    return pl.pallas_call(
        _ada_kernel,
        grid=(DEPTH, 6 * D // tn),
        in_specs=[pl.BlockSpec((n, D), lambda l, j: (0, 0)),
                  pl.BlockSpec((None, D, tn), lambda l, j: (l, 0, j)),
                  pl.BlockSpec((None, 1, tn), lambda l, j: (l, 0, j))],
        out_specs=pl.BlockSpec((None, n, tn), lambda l, j: (l, 0, j)),
        out_shape=jax.ShapeDtypeStruct((DEPTH, n, 6 * D), F32),
        compiler_params=_cparams("arbitrary", "arbitrary"),
        name="ada",
    )(conds, w_ada, b_ada.reshape(DEPTH, 1, 6 * D))


IN_AB = 4 * DNW
IN_SSM = IN_AB + 128
IN_POOL = IN_SSM + SSM_W
IN_GATE = IN_POOL + POOL_W
IN_END = IN_GATE + 3 * D


def _inproj_kernel(with_pos, x_ref, *refs):
    if with_pos:
        pos_ref, mod_ref, g_ref, w_ref, qkvz_ref, ab_ref, ussm_ref, upool_ref, gate_ref, x0_ref = refs
        x = x_ref[...] + pos_ref[...]
        x0_ref[...] = x
    else:
        mod_ref, g_ref, w_ref, qkvz_ref, ab_ref, ussm_ref, upool_ref, gate_ref = refs
        x = x_ref[...]
    y = _rms(x) * g_ref[...]
    h = (y * (1.0 + mod_ref[0, 1:2, :]) + mod_ref[0, 0:1, :]).astype(BF16)
    qkvz_ref[...] = jnp.dot(h, w_ref[:, 0:IN_AB], preferred_element_type=F32)
    ab_ref[...] = jnp.dot(h, w_ref[:, IN_AB:IN_SSM], preferred_element_type=F32)
    ussm_ref[...] = jnp.dot(h, w_ref[:, IN_SSM:IN_POOL], preferred_element_type=F32)
    upool_ref[...] = jnp.dot(h, w_ref[:, IN_POOL:IN_GATE], preferred_element_type=F32)
    gate_ref[...] = jnp.dot(h, w_ref[:, IN_GATE:IN_END], preferred_element_type=F32).astype(BF16)


TOKEN_TILE = 512


def _pack_w_in(w_in):
    n_ab = 4 * HEADS
    return jnp.concatenate([w_in[:, :IN_AB], w_in[:, IN_AB:IN_AB + n_ab], jnp.zeros((D, 128 - n_ab), F32),
                            w_in[:, IN_AB + n_ab:]], axis=1).astype(BF16)


def _mod_map(mod, seq_len, tm):
    if mod.shape[0] == 1:
        return lambda i: (0, 0, 0)
    assert seq_len % tm == 0
    return lambda i: (i // (seq_len // tm), 0, 0)


def _in_proj(x2, mod, seq_len, norm_g, w, pos=None):
    t = x2.shape[0]
    tm = TOKEN_TILE
    mod_map = _mod_map(mod, seq_len, tm)
    row = lambda w: pl.BlockSpec((tm, w), lambda i: (i, 0))
    in_specs = [row(D), pl.BlockSpec((1, 6, D), mod_map), _const_spec((1, D)), _const_spec(w.shape)]
    out_specs = [row(4 * DNW), row(128), row(SSM_W), row(POOL_W), row(3 * D)]
    out_shape = [jax.ShapeDtypeStruct((t, 4 * DNW), F32), jax.ShapeDtypeStruct((t, 128), F32),
                 jax.ShapeDtypeStruct((t, SSM_W), F32), jax.ShapeDtypeStruct((t, POOL_W), F32),
                 jax.ShapeDtypeStruct((t, 3 * D), BF16)]
    args = [x2, mod, norm_g.reshape(1, D), w]
    if pos is not None:
        assert seq_len % tm == 0
        in_specs.insert(1, pl.BlockSpec((tm, D), lambda i: (i % (seq_len // tm), 0)))
        args.insert(1, pos)
        out_specs.append(row(D))
        out_shape.append(jax.ShapeDtypeStruct((t, D), F32))
    outs = pl.pallas_call(
        functools.partial(_inproj_kernel, pos is not None),
        grid=(t // tm,),
        in_specs=in_specs,
        out_specs=out_specs,
        out_shape=out_shape,
        compiler_params=_cparams("arbitrary"),
        name="in_proj",
    )(*args)
    return (outs[5] if pos is not None else x2, *outs[:5])


def _dn_kernel(qkv_ref, z_ref, ab_ref, cw_ref, alog_ref, dtb_ref, ng_ref, s0_ref,
               o_ref, sfin_ref,
               pad_s, q_s, k_s, v_s, gc_s, gr_s, b_s, o_s, st_s, lm_s):
    L = o_ref.shape[0]
    nc = L // CH

    ab = ab_ref[...]
    xs = ab + dtb_ref[...]
    softplus = jnp.maximum(xs, 0.0) + jnp.log1p(jnp.exp(-jnp.abs(xs)))
    g = -jnp.exp(alog_ref[...]) * softplus
    gt = g.T
    b_s[...] = jax.nn.sigmoid(ab)
    ri = lax.broadcasted_iota(jnp.int32, (CH, CH), 0)
    ci = lax.broadcasted_iota(jnp.int32, (CH, CH), 1)
    lo = (ri >= ci).astype(F32)
    up = (ri <= ci).astype(F32)
    for c in range(nc):
        r = slice(c * CH, (c + 1) * CH)
        for d, (tri, tri_t) in enumerate(((lo, up), (up, lo))):
            gc_s[d, r, :] = _mm_exact(tri, g[r, :])
            gr_s[d, :, r] = _mm_exact(gt[:, r], tri_t)

    pad_s[0:PAD, :] = jnp.zeros((PAD, 3 * DNW), F32)
    pad_s[PAD + L:PAD + L + PAD, :] = jnp.zeros((PAD, 3 * DNW), F32)
    pad_s[PAD:PAD + L, :] = qkv_ref[...]
    rc = 256
    for cb in range(3 * HEADS):
        cols = slice(cb * HD, (cb + 1) * HD)
        w = cw_ref[:, cols]
        for r0 in range(0, L, rc):
            y = (w[0:1] * pad_s[PAD - 1 + r0:PAD - 1 + r0 + rc, cols]
                 + w[1:2] * pad_s[PAD + r0:PAD + r0 + rc, cols]
                 + w[2:3] * pad_s[PAD + 1 + r0:PAD + 1 + r0 + rc, cols])
            y = _silu(y)
            h = cb % HEADS
            hc = slice(h * HD, (h + 1) * HD)
            if cb < HEADS:
                y = y * lax.rsqrt(jnp.sum(y * y, axis=-1, keepdims=True) + EPS) * (HD ** -0.5)
                q_s[r0:r0 + rc, hc] = y
            elif cb < 2 * HEADS:
                y = y * lax.rsqrt(jnp.sum(y * y, axis=-1, keepdims=True) + EPS)
                k_s[r0:r0 + rc, hc] = y
            else:
                v_s[r0:r0 + rc, hc] = y

    for d in range(2):
        for h in range(HEADS):
            st_s[d * HEADS + h] = s0_ref[d, h]
    o_s[...] = jnp.zeros(o_s.shape, F32)

    eye = (ri == ci).astype(F32)
    for lvl in range(N_LVL):
        pair = ((ri >> (lvl + 1)) == (ci >> (lvl + 1))) & ((ri >> lvl) != (ci >> lvl))
        lm_s[lvl] = pair.astype(F32)

    def chunk_steps(cs):
        incl = [ri >= ci, ri <= ci]
        strict = [ri > ci, ri < ci]
        last = [CH - 1, 0]
        hcs = [slice(h * HD, (h + 1) * HD) for h in range(HEADS)]
        rows, gcol, grow = {}, {}, {}
        for n, c in enumerate(cs):
            for d in range(2):
                cidx = c if d == 0 else nc - 1 - c
                r = pl.ds(pl.multiple_of(cidx * CH, CH), CH)
                rows[n, d] = r
                gcol[n, d] = gc_s[d, r, :]
                grow[n, d] = gr_s[d, :, r]
        keys = [(n, d, h) for n in range(len(cs)) for d in range(2) for h in range(HEADS)]
        q, k, v, gc, gtot, beta, decay, eg, kb = ({} for _ in range(9))
        for key in keys:
            n, d, h = key
            s = d * HEADS + h
            r = rows[n, d]
            q[key] = q_s[r, hcs[h]]
            k[key] = k_s[r, hcs[h]]
            v[key] = v_s[r, hcs[h]]
            gc[key] = gcol[n, d][:, s:s + 1]
            gtot[key] = gcol[n, d][last[d]:last[d] + 1, s:s + 1]
            beta[key] = b_s[r, 2 * HEADS + s:2 * HEADS + s + 1]
            decay[key] = jnp.where(incl[d], jnp.exp(gc[key] - grow[n, d][s:s + 1, :]), 0.0)
            eg[key] = jnp.exp(gc[key])
            kb[key] = k[key] * beta[key]
        raw = {key: _mm_nt(jnp.concatenate([kb[key], q[key]], axis=0), k[key]) for key in keys}
        a = {key: jnp.where(strict[key[1]], raw[key][:CH] * decay[key], 0.0) for key in keys}
        qk = {key: raw[key][CH:] * decay[key] for key in keys}
        t = {key: eye - a[key] * lm_s[0] for key in keys}
        for lvl in range(1, N_LVL):
            x = {key: _mm(a[key] * lm_s[lvl], t[key]) for key in keys}
            t = {key: t[key] - _mm(t[key], x[key]) for key in keys}
        uw = {key: _mm(t[key], jnp.concatenate([v[key] * beta[key], kb[key] * eg[key]], axis=1)) for key in keys}
        for n in range(len(cs)):
            step = [key for key in keys if key[0] == n]
            st = {key: st_s[key[1] * HEADS + key[2]] for key in step}
            ws = {key: _mm(jnp.concatenate([uw[key][:, HD:], q[key] * eg[key]], axis=0), st[key]) for key in step}
            v_new = {key: uw[key][:, :HD] - ws[key][:CH] for key in step}
            o = {key: ws[key][CH:] + _mm(qk[key], v_new[key]) for key in step}
            for key in step:
                st_s[key[1] * HEADS + key[2]] = (st[key] * jnp.exp(gtot[key])
                                                 + _mm_tn(k[key] * jnp.exp(gtot[key] - gc[key]), v_new[key]))
            for key in step:
                r, hc = rows[n, key[1]], hcs[key[2]]
                o_s[r, hc] = o_s[r, hc] + o[key]

    steps_per_iter = 2
    assert nc % steps_per_iter == 0

    def chunk_iter(it, carry):
        chunk_steps([it * steps_per_iter + n for n in range(steps_per_iter)])
        return carry

    lax.fori_loop(0, nc // steps_per_iter, chunk_iter, 0)

    for d in range(2):
        for h in range(HEADS):
            sfin_ref[d, h] = st_s[d * HEADS + h]
    for h in range(HEADS):
        hc = slice(h * HD, (h + 1) * HD)
        for r0 in range(0, L, rc):
            z = z_ref[r0:r0 + rc, hc]
            o_ref[r0:r0 + rc, hc] = (_rms(o_s[r0:r0 + rc, hc]) * ng_ref[...] * _silu(z)).astype(BF16)


def _deltanet(qkvz, ab, n_seq, seq_len, conv_w, a_log, dt_bias, norm_g, s0):
    L = seq_len
    pad_row = lambda v: jnp.zeros((1, 128), F32).at[0, :2 * HEADS].set(v.reshape(-1))
    st_spec = pl.BlockSpec((None, 2, HEADS, HD, HD), lambda i: (i, 0, 0, 0, 0))
    return pl.pallas_call(
        _dn_kernel,
        grid=(n_seq,),
        in_specs=[pl.BlockSpec((L, 3 * DNW), lambda i: (i, 0)),
                  pl.BlockSpec((L, DNW), lambda i: (i, 3)),
                  pl.BlockSpec((L, 128), lambda i: (i, 0)),
                  _const_spec((3, 3 * DNW)), _const_spec((1, 128)), _const_spec((1, 128)),
                  _const_spec((1, HD)), st_spec],
        out_specs=[pl.BlockSpec((L, DNW), lambda i: (i, 0)), st_spec],
        out_shape=[jax.ShapeDtypeStruct((n_seq * L, DNW), BF16),
                   jax.ShapeDtypeStruct((n_seq, 2, HEADS, HD, HD), F32)],
        scratch_shapes=[pltpu.VMEM((L + 2 * PAD, 3 * DNW), F32),
                        pltpu.VMEM((L, DNW), F32), pltpu.VMEM((L, DNW), F32), pltpu.VMEM((L, DNW), F32),
                        pltpu.VMEM((2, L, 128), F32), pltpu.VMEM((2, 128, L), F32), pltpu.VMEM((L, 128), F32),
                        pltpu.VMEM((L, DNW), F32), pltpu.VMEM((2 * HEADS, HD, HD), F32),
                        pltpu.VMEM((N_LVL, CH, CH), F32)],
        compiler_params=_cparams("arbitrary"),
        name="deltanet",
    )(qkvz, qkvz, ab, conv_w, pad_row(a_log), pad_row(dt_bias), norm_g.reshape(1, HD), s0)


def _s5disc_kernel(lre_ref, lim_ref, ls_ref, bre_ref, bim_ref, lam_ref, bb_ref):
    lam_re = lre_ref[...]
    lam_im = lim_ref[...]
    step = jnp.exp(ls_ref[...])
    mag = jnp.exp(lam_re * step)
    ang = lam_im * step
    lb_re = mag * jnp.cos(ang)
    lb_im = mag * jnp.sin(ang)
    nr, ni = lb_re - 1.0, lb_im
    den = lam_re * lam_re + lam_im * lam_im
    f_re = (nr * lam_re + ni * lam_im) / den
    f_im = (ni * lam_re - nr * lam_im) / den
    lam_ref[:, 0, :] = lb_re
    lam_ref[:, 1, :] = lb_im
    b_re = bre_ref[...]
    b_im = bim_ref[...]
    for d in range(2):
        fr = f_re[d:d + 1]
        fi = f_im[d:d + 1]
        bb_ref[d, 0] = fr * b_re - fi * b_im
        bb_ref[d, 1] = fr * b_im + fi * b_re


def _s5_discretise(lam_re, lam_im, log_step, b_re, b_im):
    ls = jnp.repeat(log_step, SSM_P, axis=1)
    bt = lambda b: b.reshape(SSM_N, SSM_GC).T
    return pl.pallas_call(
        _s5disc_kernel,
        out_shape=[jax.ShapeDtypeStruct((2, 2, SSM_N), F32),
                   jax.ShapeDtypeStruct((2, 2, SSM_GC, SSM_N), F32)],
        name="s5_discretise",
    )(lam_re.reshape(2, SSM_N), lam_im.reshape(2, SSM_N), ls, bt(b_re), bt(b_im))


def _s5_block_mats(bb, c_re, c_im):
    eye = jnp.eye(SUB, dtype=F32)
    b6 = bb.reshape(2, 2, SSM_GC, SSM_BLK, SUB, SSM_P)
    bd = jnp.einsum("drcjgp,gh->djgcrhp", b6, eye).reshape(2, SSM_BLK, SUB * SSM_GC, 2 * SUB * SSM_P)
    c2 = jnp.stack([c_re, -c_im], axis=0).reshape(2, SSM_BLK, SUB, SSM_GC, SSM_P)
    cm = jnp.einsum("rjgcp,gh->jrgphc", c2, eye).reshape(SSM_BLK, 2 * SUB * SSM_P, SUB * SSM_GC)
    return bd.astype(BF16), cm.astype(BF16)


def _s5_kernel(uf_ref, ub_ref, bd_ref, cm_ref, lam_ref, h0_ref, yf_ref, yb_ref, fin_ref,
               xf_s, xb_s, uf_s, ub_s, yf_s, yb_s, car_s):
    i = pl.program_id(1)
    tl = uf_ref.shape[1]
    rows = tl * SUB
    bw = 2 * SUB * SSM_P
    half = SUB * SSM_P

    @pl.when(i == 0)
    def _():
        car_s[...] = h0_ref[...]

    def seq_rows(s):
        return pl.ds(s, tl, stride=SUB)

    blk_w = SUB * SSM_GC

    def expand(d, u_ref, ut_s, x_s):
        for s in range(SUB):
            for j in range(SSM_BLK):
                ut_s[j, seq_rows(s), :] = u_ref[s, :, j * blk_w:(j + 1) * blk_w]
        for j in range(SSM_BLK):
            x_s[:, j * bw:(j + 1) * bw] = jnp.dot(ut_s[j].astype(BF16), bd_ref[d, j],
                                                  preferred_element_type=F32)

    def scan(d, x_s):
        lw = 256
        for j in range(SSM_BLK):
            for kk in range(half // lw):
                cre = slice(j * bw + kk * lw, j * bw + (kk + 1) * lw)
                cim = slice(j * bw + half + kk * lw, j * bw + half + (kk + 1) * lw)
                n0 = j * half + kk * lw
                lr = jnp.broadcast_to(lam_ref[d, 0:1, n0:n0 + lw], (SUB, lw))
                li = jnp.broadcast_to(lam_ref[d, 1:2, n0:n0 + lw], (SUB, lw))
                xr = car_s[d, :, cre]
                xi = car_s[d, :, cim]
                for t in (range(tl) if d == 0 else range(tl - 1, -1, -1)):
                    r = slice(t * SUB, (t + 1) * SUB)
                    xr, xi = lr * xr - li * xi + x_s[r, cre], lr * xi + li * xr + x_s[r, cim]
                    x_s[r, cre] = xr
                    x_s[r, cim] = xi
                car_s[d, :, cre] = xr
                car_s[d, :, cim] = xi

    def project(x_s, yt_s, y_ref):
        for j in range(SSM_BLK):
            yt_s[j] = jnp.dot(x_s[:, j * bw:(j + 1) * bw].astype(BF16), cm_ref[j], preferred_element_type=F32)
        for s in range(SUB):
            for j in range(SSM_BLK):
                y_ref[s, :, j * blk_w:(j + 1) * blk_w] = yt_s[j, seq_rows(s), :].astype(BF16)

    expand(0, uf_ref, uf_s, xf_s)
    scan(0, xf_s)
    expand(1, ub_ref, ub_s, xb_s)
    project(xf_s, yf_s, yf_ref)
    scan(1, xb_s)
    project(xb_s, yb_s, yb_ref)

    @pl.when(i == pl.num_programs(1) - 1)
    def _():
        fin_ref[...] = car_s[...]


def _s5_scan(u, bd, cm, lam, h0):
    n_seq, L = u.shape[0], u.shape[1]
    n_grp = n_seq // SUB
    tl = 32
    nt = L // tl
    ncol = 2 * SSM_N
    fwd = pl.BlockSpec((SUB, tl, SSM_W), lambda g, i: (g, i, 0))
    bwd = pl.BlockSpec((SUB, tl, SSM_W), lambda g, i: (g, nt - 1 - i, 0))
    st = pl.BlockSpec((None, 2, SUB, ncol), lambda g, i: (g, 0, 0, 0))
    tile = lambda w: pltpu.VMEM((tl * SUB, w), F32)
    relayout = pltpu.VMEM((SSM_BLK, tl * SUB, SUB * SSM_GC), F32)
    return pl.pallas_call(
        _s5_kernel,
        grid=(n_grp, nt),
        in_specs=[fwd, bwd, _const_spec(bd.shape), _const_spec(cm.shape), _const_spec(lam.shape), st],
        out_specs=[fwd, bwd, st],
        out_shape=[jax.ShapeDtypeStruct(u.shape, BF16), jax.ShapeDtypeStruct(u.shape, BF16),
                   jax.ShapeDtypeStruct((n_grp, 2, SUB, ncol), F32)],
        scratch_shapes=[tile(ncol), tile(ncol), relayout, relayout, relayout, relayout,
                        pltpu.VMEM((2, SUB, ncol), F32)],
        compiler_params=_cparams("arbitrary", "arbitrary"),
        name="s5_scan",
    )(u, u, bd, cm, lam, h0)


def _state_to_cols(h_re, h_im):
    b = h_re.shape[0]
    r = h_re.reshape(b, 2, SSM_BLK, SUB * SSM_P)
    m = h_im.reshape(b, 2, SSM_BLK, SUB * SSM_P)
    return jnp.stack([r, m], axis=3).reshape(b, 2, 2 * SSM_N)


def _cols_to_state(cols):
    b = cols.shape[0]
    c = cols.reshape(b, 2, SSM_BLK, 2, SUB * SSM_P)
    return (c[:, :, :, 0].reshape(b, 2, SSM_G, SSM_P), c[:, :, :, 1].reshape(b, 2, SSM_G, SSM_P))


def _merge_kernel(seq_len, x_ref, odn_ref, yf_ref, yb_ref, ussm_ref, up_ref, up_prev_ref, up_next_ref,
                  gate_ref, mod_ref, sd_ref, gw_ref, gb_ref, pw_ref, ps_ref, wdn_ref, wssm_ref, wpool_ref,
                  wout_ref, o_ref, pad_s, pool_s):
    tm = x_ref.shape[0]
    L = seq_len

    tile_pos = lax.rem(pl.program_id(0) * tm, L)
    interior = tm > L
    if interior:
        pad_s[0:PAD, :] = jnp.zeros((PAD, POOL_W), F32)
        pad_s[PAD + tm:PAD + tm + PAD, :] = jnp.zeros((PAD, POOL_W), F32)
    else:
        pad_s[0:PAD, :] = jnp.where(tile_pos == 0, 0.0, up_prev_ref[...])
        pad_s[PAD + tm:PAD + tm + PAD, :] = jnp.where(tile_pos + tm == L, 0.0, up_next_ref[...])
    pad_s[PAD:PAD + tm, :] = up_ref[...]
    rc = 256
    for gi, w in enumerate(POOL_WINDOWS):
        cols = slice(gi * POOL_G, (gi + 1) * POOL_G)
        for r0 in range(0, tm, rc):
            p = lax.rem(tile_pos + r0 + lax.broadcasted_iota(jnp.int32, (rc, POOL_G), 0), L)
            acc = jnp.zeros((rc, POOL_G), F32)
            for dlt in range(-(w // 2), w // 2):
                term = pad_s[PAD + r0 + dlt:PAD + r0 + dlt + rc, cols]
                if interior:
                    term = jnp.where((p + dlt >= 0) if dlt < 0 else (p + dlt < L), term, 0.0)
                acc = acc + term
            cnt = jnp.minimum(p + w // 2, L) - jnp.maximum(p - w // 2, 0)
            pooled = acc / cnt.astype(F32) - pad_s[PAD + r0:PAD + r0 + rc, cols]
            pool_s[r0:r0 + rc, cols] = pooled.astype(BF16)

    y = yf_ref[...].astype(F32) + yb_ref[...].astype(F32) + sd_ref[...] * ussm_ref[...]
    y = jax.nn.gelu(y)
    o_ssm = y * jax.nn.sigmoid(_mm(y, gw_ref[...]) + gb_ref[...])
    o_pool = jnp.concatenate(
        [_mm(pool_s[:, gi * POOL_G:(gi + 1) * POOL_G], pw_ref[gi]) for gi in range(len(POOL_WINDOWS))],
        axis=1) * ps_ref[...]
    gate = lambda i: jax.nn.sigmoid(gate_ref[:, i * D:(i + 1) * D].astype(F32))
    merged = (gate(0) * _mm(odn_ref[...], wdn_ref[...]) + gate(1) * _mm(o_ssm, wssm_ref[...])
              + gate(2) * _mm(o_pool, wpool_ref[...]))
    o_ref[...] = x_ref[...] + mod_ref[0, 2:3, :] * _mm(merged, wout_ref[...])


def _merge(x2, odn, yf, yb, u_ssm, u_pool, gates, mod, seq_len, ssm_d, glu_w, glu_b, pool_w, pool_scale,
           wdn, wssm, wpool, wout):
    t = x2.shape[0]
    tm = TOKEN_TILE
    assert tm % seq_len == 0 or seq_len % tm == 0
    mod_map = _mod_map(mod, seq_len, tm)
    row = lambda w: pl.BlockSpec((tm, w), lambda i: (i, 0))
    per_tile = tm // PAD
    prev = pl.BlockSpec((PAD, POOL_W), lambda i: (jnp.maximum(i * per_tile - 1, 0), 0))
    nxt = pl.BlockSpec((PAD, POOL_W), lambda i: (jnp.minimum((i + 1) * per_tile, t // PAD - 1), 0))
    return pl.pallas_call(
        functools.partial(_merge_kernel, seq_len),
        grid=(t // tm,),
        in_specs=[row(D), row(DNW), row(SSM_W), row(SSM_W), row(SSM_W), row(POOL_W), prev, nxt, row(3 * D),
                  pl.BlockSpec((1, 6, D), mod_map),
                  _const_spec((1, SSM_W)), _const_spec(glu_w.shape), _const_spec((1, SSM_W)),
                  _const_spec(pool_w.shape), _const_spec((1, POOL_W)),
                  _const_spec(wdn.shape), _const_spec(wssm.shape), _const_spec(wpool.shape),
                  _const_spec(wout.shape)],
        out_specs=row(D),
        out_shape=jax.ShapeDtypeStruct((t, D), F32),
        scratch_shapes=[pltpu.VMEM((tm + 2 * PAD, POOL_W), F32), pltpu.VMEM((tm, POOL_W), BF16)],
        compiler_params=_cparams("arbitrary"),
        name="merge",
    )(x2, odn, yf, yb, u_ssm, u_pool, u_pool, u_pool, gates, mod, ssm_d.reshape(1, SSM_W), glu_w,
      glu_b.reshape(1, SSM_W), pool_w, pool_scale.reshape(1, POOL_W), wdn, wssm, wpool, wout)


def _ffn_kernel(n_seq, final, x_ref, mod_ref, g_ref, wup_ref, cw_ref, wdn_ref, fg_ref, o_ref,
                h_s, pad0_s, pad1_s, pad2_s, act0_s, act1_s, acc_s):
    rows = x_ref.shape[0]
    L = rows // n_seq
    tm = 256
    pad_s = (pad0_s, pad1_s, pad2_s)
    act_s = (act0_s, act1_s)

    def norm_tile(it, carry):
        rs = pl.ds(pl.multiple_of(it * tm, tm), tm)
        y = _rms(x_ref[rs, :]) * g_ref[...]
        h_s[rs, :] = (y * (1.0 + mod_ref[0, 4:5, :]) + mod_ref[0, 3:4, :]).astype(BF16)
        return carry

    lax.fori_loop(0, rows // tm, norm_tile, 0)

    for slot in range(len(pad_s)):
        for n in range(n_seq):
            pad_s[slot][n, 0:PAD, :] = jnp.zeros((PAD, 2 * FF_BLK), F32)
            pad_s[slot][n, PAD + L:PAD + L + PAD, :] = jnp.zeros((PAD, 2 * FF_BLK), F32)
    acc_s[...] = jnp.zeros(acc_s.shape, F32)
    rc = min(L, 256)

    def gate_cols(j):
        return slice(j * FF_BLK, (j + 1) * FF_BLK)

    def val_cols(j):
        return slice(DFF + j * FF_BLK, DFF + (j + 1) * FF_BLK)

    ur = min(L, 512)

    def up(j, slot):
        for n in range(n_seq):
            for r0 in range(0, L, ur):
                h = h_s[n * L + r0:n * L + r0 + ur, :]
                pad_s[slot][n, PAD + r0:PAD + r0 + ur, 0:FF_BLK] = jnp.dot(
                    h, wup_ref[:, gate_cols(j)], preferred_element_type=F32)
                pad_s[slot][n, PAD + r0:PAD + r0 + ur, FF_BLK:2 * FF_BLK] = jnp.dot(
                    h, wup_ref[:, val_cols(j)], preferred_element_type=F32)

    def conv_act(j, slot, act_slot):
        wg = cw_ref[:, gate_cols(j)]
        wv = cw_ref[:, val_cols(j)]
        for n in range(n_seq):
            for r0 in range(0, L, rc):
                def conv(w, cols):
                    return (w[0:1] * pad_s[slot][n, PAD - 1 + r0:PAD - 1 + r0 + rc, cols]
                            + w[1:2] * pad_s[slot][n, PAD + r0:PAD + r0 + rc, cols]
                            + w[2:3] * pad_s[slot][n, PAD + 1 + r0:PAD + 1 + r0 + rc, cols])
                hg = conv(wg, slice(0, FF_BLK))
                hv = conv(wv, slice(FF_BLK, 2 * FF_BLK))
                act_s[act_slot][n * L + r0:n * L + r0 + rc, :] = (_silu(hg) * hv).astype(BF16)

    def down(j, slot):
        acc_s[...] += jnp.dot(act_s[slot][...], wdn_ref[gate_cols(j), :], preferred_element_type=F32)

    n_blk = DFF // FF_BLK
    look = len(pad_s)
    for j in range(min(look, n_blk)):
        up(j, j % look)
    for j in range(n_blk):
        conv_act(j, j % look, j % 2)
        down(j, j % 2)
        if j + look < n_blk:
            up(j + look, j % look)

    def out_tile(it, carry):
        rs = pl.ds(pl.multiple_of(it * tm, tm), tm)
        y = x_ref[rs, :] + mod_ref[0, 5:6, :] * acc_s[rs, :]
        if final:
            y = _rms(y) * fg_ref[...]
        o_ref[rs, :] = y
        return carry

    lax.fori_loop(0, rows // tm, out_tile, 0)


def _ffn(x2, mod, n_tile_seq, seq_len, norm_g, wup, conv_w, wdn, final_g, final):
    t = x2.shape[0]
    rows = n_tile_seq * seq_len
    per_seq = mod.shape[0] > 1
    mod_map = (lambda i: (i, 0, 0)) if per_seq else (lambda i: (0, 0, 0))
    if per_seq:
        assert n_tile_seq == 1
    row = pl.BlockSpec((rows, D), lambda i: (i, 0))
    return pl.pallas_call(
        functools.partial(_ffn_kernel, n_tile_seq, final),
        grid=(t // rows,),
        in_specs=[row, pl.BlockSpec((1, 6, D), mod_map), _const_spec((1, D)), _const_spec(wup.shape),
                  _const_spec(conv_w.shape), _const_spec(wdn.shape), _const_spec((1, D))],
        out_specs=row,
        out_shape=jax.ShapeDtypeStruct((t, D), F32),
        scratch_shapes=[pltpu.VMEM((rows, D), BF16),
                        pltpu.VMEM((n_tile_seq, seq_len + 2 * PAD, 2 * FF_BLK), F32),
                        pltpu.VMEM((n_tile_seq, seq_len + 2 * PAD, 2 * FF_BLK), F32),
                        pltpu.VMEM((n_tile_seq, seq_len + 2 * PAD, 2 * FF_BLK), F32),
                        pltpu.VMEM((rows, FF_BLK), BF16), pltpu.VMEM((rows, FF_BLK), BF16),
                        pltpu.VMEM((rows, D), F32)],
        compiler_params=_cparams("arbitrary"),
        name="conv_ffn",
    )(x2, mod, norm_g.reshape(1, D), wup, conv_w, wdn, final_g.reshape(1, D))


def _grid_pos_embed(rows, dim):
    quarter = dim // 4
    omega = 1.0 / (10000.0 ** (jnp.arange(quarter, dtype=F32) / quarter))
    r = jnp.broadcast_to(jnp.arange(rows, dtype=F32)[:, None], (rows, GRID_W)).reshape(-1)
    col = jnp.broadcast_to(jnp.arange(GRID_W, dtype=F32)[None, :], (rows, GRID_W)).reshape(-1)

    def sincos(p):
        ang = p[:, None] * omega[None, :]
        return jnp.concatenate([jnp.sin(ang), jnp.cos(ang)], axis=-1)

    return jnp.concatenate([sincos(r), sincos(col)], axis=-1)


def _layer(x2, mod, n_seq, seq_len, n_tile_seq, s_dn, h_cols, p, final, pos=None):
    x2, qkvz, ab, u_ssm, u_pool, gates = _in_proj(x2, mod, seq_len, p["norm1_g"], p["w_in"], pos)
    odn, st_dn = _deltanet(qkvz, ab, n_seq, seq_len, p["dn_conv"], p["dn_a_log"], p["dn_dt_bias"],
                           p["dn_norm_g"], s_dn)
    h0 = h_cols.reshape(n_seq // SUB, SUB, 2, 2 * SSM_N).transpose(0, 2, 1, 3)
    yf, yb, fin = _s5_scan(u_ssm.reshape(n_seq, seq_len, SSM_W), p["bd"], p["cm"], p["lam"], h0)
    yf = yf.reshape(n_seq * seq_len, SSM_W)
    yb = yb.reshape(n_seq * seq_len, SSM_W)
    fin = fin.transpose(0, 2, 1, 3).reshape(n_seq, 2, 2 * SSM_N)
    x2 = _merge(x2, odn, yf, yb, u_ssm, u_pool, gates, mod, seq_len, p["ssm_d"], p["glu_w"], p["ssm_glu_b"],
                p["pool_w"], p["pool_scale"], p["wdn"], p["wssm"], p["wpool"], p["wout"])
    x2 = _ffn(x2, mod, n_tile_seq, seq_len, p["norm2_g"], p["wup"], p["ffn_conv"], p["wdown"],
              p["final_norm_g"], final)
    return x2, st_dn, fin


def kernel(x_prompt, x_sample, state_dn, state_ssm_re, state_ssm_im, c, c_ctx, norm1_g, norm2_g, w_ada, b_ada, w_in, dn_conv, dn_a_log, dn_dt_bias, dn_norm_g, ssm_lambda_re, ssm_lambda_im, ssm_log_step, ssm_b_re, ssm_b_im, ssm_c_re, ssm_c_im, ssm_d, ssm_glu_w, ssm_glu_b, pool_w, pool_scale, w_branch_dn, w_branch_ssm, w_branch_pool, w_out, ffn_w_up, ffn_conv, ffn_w_down, final_norm_g):
    bc, lc, _ = x_prompt.shape
    bl, ll, _ = x_sample.shape

    n_cond = 16
    conds = jnp.zeros((n_cond, D), F32).at[0].set(c_ctx).at[1:1 + bl].set(c)
    mods = _ada(conds, w_ada, b_ada).reshape(DEPTH, n_cond, 6, D)

    x_ctx = x_prompt.reshape(bc * lc, D)
    x_lat = x_sample.reshape(bl * ll, D)
    pos = _grid_pos_embed(ll // GRID_W, D)

    zero_dn = jnp.zeros((bc, 2, HEADS, HD, HD), F32)
    zero_h = jnp.zeros((bc, 2, 2 * SSM_N), F32)
    new_dn, new_re, new_im = [], [], []
    for l in range(DEPTH):
        lam, bb = _s5_discretise(ssm_lambda_re[l], ssm_lambda_im[l], ssm_log_step[l], ssm_b_re[l], ssm_b_im[l])
        bd, cm = _s5_block_mats(bb, ssm_c_re[l], ssm_c_im[l])
        p = dict(
            norm1_g=norm1_g[l], norm2_g=norm2_g[l], final_norm_g=final_norm_g, w_in=_pack_w_in(w_in[l]),
            dn_conv=dn_conv[l], dn_a_log=dn_a_log[l], dn_dt_bias=dn_dt_bias[l], dn_norm_g=dn_norm_g[l],
            bd=bd, cm=cm, lam=lam, ssm_d=ssm_d[l], glu_w=ssm_glu_w[l].astype(BF16), ssm_glu_b=ssm_glu_b[l],
            pool_w=pool_w[l].astype(BF16), pool_scale=pool_scale[l],
            wdn=w_branch_dn[l].astype(BF16), wssm=w_branch_ssm[l].astype(BF16),
            wpool=w_branch_pool[l].astype(BF16), wout=w_out[l].astype(BF16),
            wup=ffn_w_up[l].astype(BF16), ffn_conv=ffn_conv[l], wdown=ffn_w_down[l].astype(BF16),
        )
        final = l == DEPTH - 1
        x_ctx, st_dn, fin = _layer(x_ctx, mods[l, 0:1], bc, lc, 4, zero_dn, zero_h, p, final)
        new_dn.append(st_dn)
        f_re, f_im = _cols_to_state(fin)
        new_re.append(f_re)
        new_im.append(f_im)
        h_lat = _state_to_cols(state_ssm_re[:, l], state_ssm_im[:, l])
        x_lat, _, _ = _layer(x_lat, mods[l, 1:1 + bl], bl, ll, 1, state_dn[:, l], h_lat, p, final,
                             pos if l == 0 else None)

    return (x_ctx.reshape(bc, lc, D), x_lat.reshape(bl, ll, D), jnp.stack(new_dn, axis=1),
            jnp.stack(new_re, axis=1), jnp.stack(new_im, axis=1))
```
